```python
import math, functools
import jax, jax.numpy as jnp
from jax import lax
import numpy as np

D_MODEL = 2048
BATCH = 1
SEQ = 8192
DEPTH = 2
DEC_BATCH = 128
DEC_SEQ = 8
PAST_LEN = 16384
PAGE_SIZE = 128

D_RNN = D_MODEL // 2
RNN_BLOCKS = 8
RNN_BLOCK = D_RNN // RNN_BLOCKS
CONV_W = 4
LRU_C = 8.0
N_HEADS = 8
NOPE_DIM = 128
ROPE_DIM = 64
QK_DIM = NOPE_DIM + ROPE_DIM
V_DIM = 128
Q_RANK = D_MODEL // 4
KV_RANK = D_MODEL // 8
D_ATTN = N_HEADS * V_DIM
D_MIX = D_RNN + D_ATTN
D_IN = 2 * D_RNN + Q_RANK + KV_RANK + ROPE_DIM
D_FF = 4 * D_MODEL
ROPE_THETA = 10000.0
Q_BLOCK = 128
EPS = 1e-6
SCALE = QK_DIM ** -0.5

kernel_name = 'hymba_rglru_mla_decoder_step'


def rms_norm(x, g):
    xf = x.astype(jnp.float32)
    xf = xf * lax.rsqrt(jnp.mean(xf * xf, axis=-1, keepdims=True) + EPS)
    return (xf * g.astype(jnp.float32)).astype(x.dtype)


def rope(x, pos):
    half = ROPE_DIM // 2
    inv_freq = ROPE_THETA ** (-jnp.arange(half, dtype=jnp.float32) / half)
    ang = pos.astype(jnp.float32)[:, None] * inv_freq[None, :]
    cos = jnp.cos(ang)[:, None, :]
    sin = jnp.sin(ang)[:, None, :]
    xf = x.astype(jnp.float32)
    x1, x2 = xf[..., :half], xf[..., half:]
    return jnp.concatenate([x1 * cos - x2 * sin, x2 * cos + x1 * sin], axis=-1).astype(x.dtype)


def head_norm_rope(t, pos, g):
    t = rms_norm(t, g)
    return jnp.concatenate([t[..., :NOPE_DIM], rope(t[..., NOPE_DIM:], pos)], axis=-1)


def mla_keys(ckv, kpe, pos, w_uk, g_kn):
    k_nope = jnp.einsum('btr,rhd->bthd', ckv, w_uk)
    k_pe = jnp.broadcast_to(kpe[:, :, None, :], k_nope.shape[:3] + (ROPE_DIM,))
    return head_norm_rope(jnp.concatenate([k_nope, k_pe], axis=-1), pos, g_kn)


def causal_conv(u, buf, w, b):
    T = u.shape[1]
    full = jnp.concatenate([buf, u], axis=1)
    out = b + full[:, 0:T] * w[0]
    for k in range(1, CONV_W):
        out = out + full[:, k:k + T] * w[k]
    return out, full[:, T:]


def rg_lru(u, h0, w_a, b_a, w_x, b_x, lam):
    B, T, _ = u.shape
    ub = u.reshape(B, T, RNN_BLOCKS, RNN_BLOCK)
    r = jax.nn.sigmoid(jnp.einsum('btkc,kcd->btkd', ub, w_a) + b_a).reshape(B, T, D_RNN)
    i = jax.nn.sigmoid(jnp.einsum('btkc,kcd->btkd', ub, w_x) + b_x).reshape(B, T, D_RNN)
    log_a = (-LRU_C * r.astype(jnp.float32)) * jax.nn.softplus(-lam.astype(jnp.float32))
    a = jnp.exp(log_a)
    bterm = jnp.sqrt(-jnp.expm1(2.0 * log_a)) * (i * u).astype(jnp.float32)
    bterm = bterm.at[:, 0].add(a[:, 0] * h0.astype(jnp.float32))

    def combine(lhs, rhs):
        a1, b1 = lhs
        a2, b2 = rhs
        return a1 * a2, a2 * b1 + b2

    _, h = lax.associative_scan(combine, (a, bterm), axis=1)
    return h.astype(u.dtype), h[:, -1].astype(u.dtype)


def prompt_attention(q, k_new, ckv_new, w_uv):
    B, S, H, _ = q.shape
    v = jnp.einsum('btr,rhd->bthd', ckv_new, w_uv)
    nb = S // Q_BLOCK
    qb = q.reshape(B, nb, Q_BLOCK, H, QK_DIM).transpose(1, 0, 2, 3, 4)
    kpos = jnp.arange(S)

    def block(args):
        qblk, bi = args
        s = jnp.einsum('bqhd,bkhd->bhqk', qblk, k_new).astype(jnp.float32) * SCALE
        qpos = bi * Q_BLOCK + jnp.arange(Q_BLOCK)
        s = jnp.where(kpos[None, :] <= qpos[:, None], s, -jnp.inf)
        p = jax.nn.softmax(s, axis=-1).astype(v.dtype)
        return jnp.einsum('bhqk,bkhd->bqhd', p, v)

    o = lax.map(block, (qb, jnp.arange(nb)))
    return o.transpose(1, 0, 2, 3, 4).reshape(B, S, H, V_DIM)


def sample_attention(q, k_new, ckv_new, cache_ckv, cache_kpe, layer, page_table, w_uk, w_uv, g_kn):
    T = q.shape[1]
    tril = jnp.tril(jnp.ones((T, T), dtype=bool))
    s = jnp.einsum('bqhd,bkhd->bhqk', q, k_new).astype(jnp.float32) * SCALE
    s = jnp.where(tril, s, -jnp.inf)
    m = jnp.max(s, axis=-1, keepdims=True)
    p = jnp.exp(s - m)
    l = jnp.sum(p, axis=-1, keepdims=True)
    acc = jnp.einsum('bhqk,bkr->bhqr', p, ckv_new.astype(jnp.float32))
    n_pages = PAST_LEN // PAGE_SIZE

    def step(carry, j):
        m, l, acc = carry
        pages = page_table[:, j]
        ckv = cache_ckv[layer, pages]
        kpe = cache_kpe[layer, pages]
        pos = j * PAGE_SIZE + jnp.arange(PAGE_SIZE)
        k = mla_keys(ckv, kpe, pos, w_uk, g_kn)
        sp = jnp.einsum('bqhd,bkhd->bhqk', q, k).astype(jnp.float32) * SCALE
        m_new = jnp.maximum(m, jnp.max(sp, axis=-1, keepdims=True))
        corr = jnp.exp(m - m_new)
        pp = jnp.exp(sp - m_new)
        l = l * corr + jnp.sum(pp, axis=-1, keepdims=True)
        acc = acc * corr + jnp.einsum('bhqk,bkr->bhqr', pp, ckv.astype(jnp.float32))
        return (m_new, l, acc), None

    (m, l, acc), _ = lax.scan(step, (m, l, acc), jnp.arange(n_pages))
    o_lat = (acc / l).astype(q.dtype)
    return jnp.einsum('bhqr,rhd->bqhd', o_lat, w_uv)


def trunk_layer(x, conv_buf, h0, pos, attend, g_mix, w_in, conv_w, conv_b, lru_wa, lru_ba, lru_wx,
                lru_bx, lru_lambda, g_q_a, w_qb, g_kv_a, w_uk, g_qn, g_kn, g_out_rnn, g_out_attn,
                w_out, g_mlp, w_up, w_down):
    B, T, _ = x.shape
    xn = rms_norm(x, g_mix)
    z = jnp.einsum('btd,df->btf', xn, w_in)
    xr = z[..., :D_RNN]
    gate = z[..., D_RNN:2 * D_RNN]
    qa = z[..., 2 * D_RNN:2 * D_RNN + Q_RANK]
    kva = z[..., 2 * D_RNN + Q_RANK:]
    u, conv_new = causal_conv(xr, conv_buf, conv_w, conv_b)
    hseq, h_last = rg_lru(u, h0, lru_wa, lru_ba, lru_wx, lru_bx, lru_lambda)
    y_rnn = hseq * jax.nn.gelu(gate)
    ckv = rms_norm(kva[..., :KV_RANK], g_kv_a)
    kpe = kva[..., KV_RANK:]
    q = jnp.einsum('btr,rf->btf', rms_norm(qa, g_q_a), w_qb).reshape(B, T, N_HEADS, QK_DIM)
    q = head_norm_rope(q, pos, g_qn)
    k_new = mla_keys(ckv, kpe, pos, w_uk, g_kn)
    y_attn = attend(q, k_new, ckv).reshape(B, T, D_ATTN)
    mix = jnp.concatenate([rms_norm(y_rnn, g_out_rnn), rms_norm(y_attn, g_out_attn)], axis=-1)
    h = x + jnp.einsum('btf,fd->btd', mix, w_out)
    hn = rms_norm(h, g_mlp)
    y = h + jnp.einsum('btf,fd->btd', jnp.square(jax.nn.relu(jnp.einsum('btd,df->btf', hn, w_up))), w_down)
    return y, ckv, kpe, h_last, conv_new


def setup_inputs(seed: int = 0) -> dict:
    key = jax.random.key(seed)
    ks = jax.random.split(key, 32)
    f32 = jnp.float32
    n_pages = PAST_LEN // PAGE_SIZE
    n_used = DEC_BATCH * n_pages
    n_pool = (n_used * 5) // 4

    def nrm(k, shape, scale):
        return jax.random.normal(k, shape, f32) * scale

    def gain(k, shape):
        return 1.0 + 0.02 * jax.random.normal(k, shape, f32)

    a_base = jax.random.uniform(ks[14], (DEPTH, D_RNN), f32, 0.9, 0.999) ** (1.0 / LRU_C)
    page_table = jax.random.permutation(ks[6], n_pool)[:n_used].reshape(DEC_BATCH, n_pages).astype(jnp.int32)
    return {
        'x_prompt': nrm(ks[0], (BATCH, SEQ, D_MODEL), 1.0),
        'x_sample': nrm(ks[1], (DEC_BATCH, DEC_SEQ, D_MODEL), 1.0),
        'cache_ckv': nrm(ks[2], (DEPTH, n_pool, PAGE_SIZE, KV_RANK), 1.0),
        'cache_kpe': nrm(ks[3], (DEPTH, n_pool, PAGE_SIZE, ROPE_DIM), 1.0),
        'state_rglru_h': nrm(ks[4], (DEPTH, DEC_BATCH, D_RNN), 0.5),
        'state_conv': nrm(ks[5], (DEPTH, DEC_BATCH, CONV_W - 1, D_RNN), 1.0),
        'page_table': page_table,
        'g_mix': gain(ks[7], (DEPTH, D_MODEL)),
        'w_in': nrm(ks[8], (DEPTH, D_MODEL, D_IN), D_MODEL ** -0.5),
        'conv_w': nrm(ks[9], (DEPTH, CONV_W, D_RNN), CONV_W ** -0.5),
        'conv_b': nrm(ks[10], (DEPTH, D_RNN), 0.02),
        'lru_wa': nrm(ks[11], (DEPTH, RNN_BLOCKS, RNN_BLOCK, RNN_BLOCK), RNN_BLOCK ** -0.5),
        'lru_ba': nrm(ks[12], (DEPTH, RNN_BLOCKS, RNN_BLOCK), 0.02),
        'lru_wx': nrm(ks[13], (DEPTH, RNN_BLOCKS, RNN_BLOCK, RNN_BLOCK), RNN_BLOCK ** -0.5),
        'lru_bx': nrm(ks[15], (DEPTH, RNN_BLOCKS, RNN_BLOCK), 0.02),
        'lru_lambda': jnp.log(a_base / (1.0 - a_base)),
        'g_q_a': gain(ks[16], (DEPTH, Q_RANK)),
        'w_qb': nrm(ks[17], (DEPTH, Q_RANK, N_HEADS * QK_DIM), Q_RANK ** -0.5),
        'g_kv_a': gain(ks[18], (DEPTH, KV_RANK)),
        'w_uk': nrm(ks[19], (DEPTH, KV_RANK, N_HEADS, NOPE_DIM), KV_RANK ** -0.5),
        'w_uv': nrm(ks[20], (DEPTH, KV_RANK, N_HEADS, V_DIM), KV_RANK ** -0.5),
        'g_qn': gain(ks[21], (DEPTH, QK_DIM)),
        'g_kn': gain(ks[22], (DEPTH, QK_DIM)),
        'g_out_rnn': gain(ks[23], (DEPTH, D_RNN)),
        'g_out_attn': gain(ks[24], (DEPTH, D_ATTN)),
        'w_out': nrm(ks[25], (DEPTH, D_MIX, D_MODEL), D_MIX ** -0.5),
        'g_mlp': gain(ks[26], (DEPTH, D_MODEL)),
        'w_up': nrm(ks[27], (DEPTH, D_MODEL, D_FF), D_MODEL ** -0.5),
        'w_down': nrm(ks[28], (DEPTH, D_FF, D_MODEL), D_FF ** -0.5),
    }


def reference(x_prompt, x_sample, cache_ckv, cache_kpe, state_rglru_h, state_conv, page_table,
              g_mix, w_in, conv_w, conv_b, lru_wa, lru_ba, lru_wx, lru_bx, lru_lambda, g_q_a, w_qb,
              g_kv_a, w_uk, w_uv, g_qn, g_kn, g_out_rnn, g_out_attn, w_out, g_mlp, w_up, w_down):
    pos_prompt = jnp.arange(x_prompt.shape[1], dtype=jnp.int32)
    pos_sample = PAST_LEN + jnp.arange(x_sample.shape[1], dtype=jnp.int32)
    buf0 = jnp.zeros((x_prompt.shape[0], CONV_W - 1, D_RNN), x_prompt.dtype)
    h00 = jnp.zeros((x_prompt.shape[0], D_RNN), x_prompt.dtype)
    yp, ys = x_prompt, x_sample
    ckv_p, kpe_p, h_p, conv_p = [], [], [], []
    ckv_s, kpe_s, h_s, conv_s = [], [], [], []
    for l in range(DEPTH):
        lp = (g_mix[l], w_in[l], conv_w[l], conv_b[l], lru_wa[l], lru_ba[l], lru_wx[l], lru_bx[l],
              lru_lambda[l], g_q_a[l], w_qb[l], g_kv_a[l], w_uk[l], g_qn[l], g_kn[l], g_out_rnn[l],
              g_out_attn[l], w_out[l], g_mlp[l], w_up[l], w_down[l])
        attend_p = functools.partial(prompt_attention, w_uv=w_uv[l])
        yp, c1, k1, hl1, cv1 = trunk_layer(yp, buf0, h00, pos_prompt, attend_p, *lp)
        attend_s = functools.partial(sample_attention, cache_ckv=cache_ckv, cache_kpe=cache_kpe, layer=l,
                                     page_table=page_table, w_uk=w_uk[l], w_uv=w_uv[l], g_kn=g_kn[l])
        ys, c2, k2, hl2, cv2 = trunk_layer(ys, state_conv[l], state_rglru_h[l], pos_sample, attend_s, *lp)
        ckv_p.append(c1); kpe_p.append(k1); h_p.append(hl1); conv_p.append(cv1)
        ckv_s.append(c2); kpe_s.append(k2); h_s.append(hl2); conv_s.append(cv2)
    return (yp, ys,
            jnp.stack(ckv_p), jnp.stack(kpe_p), jnp.stack(h_p), jnp.stack(conv_p),
            jnp.stack(ckv_s), jnp.stack(kpe_s), jnp.stack(h_s), jnp.stack(conv_s))
```

```python
import functools
import math

import jax
import jax.numpy as jnp
from jax import lax
from jax.experimental import pallas as pl
from jax.experimental.pallas import tpu as pltpu

EPS = 1e-6
ROPE_THETA = 10000.0
LRU_C = 8.0
CONV_W = 4

SUBLANES = 8
LANES = 128
MXU_DIM = 256
VMEM_LIMIT_BYTES = 56 * 1024 * 1024

BF16 = jnp.bfloat16
F32 = jnp.float32

_NT_DIMS = (((1,), (1,)), ((), ()))


def _dot(a, b):
    return jnp.dot(a, b, preferred_element_type=F32)


def _dot_nt(a, b):
    return lax.dot_general(a, b, _NT_DIMS, preferred_element_type=F32)


def _rms(x, g):
    return x * lax.rsqrt(jnp.mean(x * x, axis=-1, keepdims=True) + EPS) * g


def _const_spec(shape):
    zeros = (0,) * len(shape)
    return pl.BlockSpec(shape, lambda *_: zeros)


def _params(semantics):
    return pltpu.CompilerParams(dimension_semantics=semantics, vmem_limit_bytes=VMEM_LIMIT_BYTES)


def _head_slot(v0, v1, g0, g1, cos, sin, qk_dim):
    ssq = jnp.sum(v0 * v0, axis=-1, keepdims=True) + jnp.sum(v1 * v1, axis=-1, keepdims=True)
    rinv = lax.rsqrt(ssq * (1.0 / qk_dim) + EPS)
    n0 = v0 * rinv * g0
    n1 = v1 * rinv * g1
    return n0, _rope_slot(n1, cos, sin)


def _rope_slot(n1, cos, sin):
    partner = pltpu.roll(n1, 32, 1) + pltpu.roll(n1, 96, 1)
    return n1 * cos + partner * sin


def _prep_kernel(x_ref, gmix_ref, win_ref, gqa_ref, wqb_ref, gkva_ref, gq0_ref, gq1_ref, gk0_ref,
                 gk1_ref, cos_ref, sin_ref, wk_ref, wv_ref, *out_refs, dims, prompt):
    d_rnn, q_rank, kv_rank, n_heads, nope, rope_dim = dims
    qk_dim = nope + rope_dim
    slot = 2 * LANES
    if prompt:
        xr_ref, gate_ref, ckv_ref, kpe_ref, q_ref, k_ref, v_ref = out_refs
    else:
        xr_ref, gate_ref, ckv_ref, kpe_ref, qlat_ref, qpe_ref = out_refs

    xn = _rms(x_ref[...], gmix_ref[...]).astype(BF16)
    o = 0
    xr_ref[...] = _dot(xn, win_ref[:, o:o + d_rnn]); o += d_rnn
    gate_ref[...] = _dot(xn, win_ref[:, o:o + d_rnn]); o += d_rnn
    qa = _dot(xn, win_ref[:, o:o + q_rank]); o += q_rank
    kvc = _dot(xn, win_ref[:, o:o + kv_rank]); o += kv_rank
    kpe = _dot(xn, win_ref[:, o:o + LANES])

    cos = cos_ref[...]
    sin = sin_ref[...]
    ckv = _rms(kvc, gkva_ref[...])
    ckv_ref[...] = ckv
    kpe_ref[...] = kpe[:, :rope_dim]

    qf = _dot(_rms(qa, gqa_ref[...]).astype(BF16), wqb_ref[...])
    gq0, gq1 = gq0_ref[...], gq1_ref[...]
    gk0, gk1 = gk0_ref[...], gk1_ref[...]
    for h in range(n_heads):
        v0 = qf[:, h * slot:h * slot + LANES]
        v1 = qf[:, h * slot + LANES:(h + 1) * slot]
        n0, r1 = _head_slot(v0, v1, gq0, gq1, cos, sin, qk_dim)
        if prompt:
            q_ref[:, h * slot:h * slot + LANES] = n0.astype(BF16)
            q_ref[:, h * slot + LANES:(h + 1) * slot] = r1.astype(BF16)
        else:
            qlat_ref[h] = _dot((n0 * gk0).astype(BF16), wk_ref[h])
            qpe_ref[h] = r1[:, :rope_dim]

    if prompt:
        ckv_bf = ckv.astype(BF16)
        v_ref[...] = _dot(ckv_bf, wv_ref[...]).astype(BF16)
        kn = _dot(ckv_bf, wk_ref[...])
        ssq_pe = jnp.sum(kpe * kpe, axis=-1, keepdims=True)
        kr = _rope_slot(kpe * gk1, cos, sin)
        for h in range(n_heads):
            kh = kn[:, h * nope:(h + 1) * nope]
            ssq = jnp.sum(kh * kh, axis=-1, keepdims=True) + ssq_pe
            rinv = lax.rsqrt(ssq * (1.0 / qk_dim) + EPS)
            k_ref[:, h * slot:h * slot + LANES] = (kh * rinv * gk0).astype(BF16)
            k_ref[:, h * slot + LANES:(h + 1) * slot] = (kr * rinv).astype(BF16)


def _prep(x, lw, cos, sin, *, prompt, tm):
    n, d_model = x.shape
    dims = lw["dims"]
    d_rnn, q_rank, kv_rank, n_heads, nope, rope_dim = dims
    slot = 2 * LANES
    assert nope == LANES and rope_dim == LANES // 2 and n % tm == 0
    row = lambda w: pl.BlockSpec((tm, w), lambda i: (i, 0))
    wk = lw["w_uk"] if prompt else lw["w_ukT_h"]
    in_specs = [row(d_model), _const_spec((1, d_model)), _const_spec(lw["w_in"].shape),
                _const_spec((1, q_rank)), _const_spec(lw["w_qb"].shape), _const_spec((1, kv_rank)),
                _const_spec((1, LANES)), _const_spec((1, LANES)), _const_spec((1, LANES)),
                _const_spec((1, LANES)), row(LANES), row(LANES), _const_spec(wk.shape),
                _const_spec(lw["w_uv"].shape)]
    out_shape = [jax.ShapeDtypeStruct((n, d_rnn), F32), jax.ShapeDtypeStruct((n, d_rnn), F32),
                 jax.ShapeDtypeStruct((n, kv_rank), F32), jax.ShapeDtypeStruct((n, rope_dim), F32)]
    out_specs = [row(d_rnn), row(d_rnn), row(kv_rank), row(rope_dim)]
    if prompt:
        out_shape += [jax.ShapeDtypeStruct((n, n_heads * slot), BF16),
                      jax.ShapeDtypeStruct((n, n_heads * slot), BF16),
                      jax.ShapeDtypeStruct((n, n_heads * nope), BF16)]
        out_specs += [row(n_heads * slot), row(n_heads * slot), row(n_heads * nope)]
    else:
        out_shape += [jax.ShapeDtypeStruct((n_heads, n, kv_rank), F32),
                      jax.ShapeDtypeStruct((n_heads, n, rope_dim), F32)]
        out_specs += [pl.BlockSpec((n_heads, tm, kv_rank), lambda i: (0, i, 0)),
                      pl.BlockSpec((n_heads, tm, rope_dim), lambda i: (0, i, 0))]
    return pl.pallas_call(
        functools.partial(_prep_kernel, dims=dims, prompt=prompt),
        grid=(n // tm,), in_specs=in_specs, out_specs=out_specs, out_shape=out_shape,
        compiler_params=_params(("parallel",)),
        name="prep_prompt" if prompt else "prep_sample",
    )(x, lw["g_mix"], lw["w_in"], lw["g_q_a"], lw["w_qb"], lw["g_kv_a"], lw["gq0"], lw["gq1"],
      lw["gk0"], lw["gk1"], cos, sin, wk, lw["w_uv"])


def _gelu_tanh(x):
    c = math.sqrt(2.0 / math.pi)
    return 0.5 * x * (1.0 + jnp.tanh(c * (x + 0.044715 * (x * x * x))))


def _rnn_kernel(x_ref, gate_ref, prev_ref, h0_ref, cw_ref, cb_ref, wax_ref, ba_ref, bx_ref, lam_ref,
                gout_ref, y_ref, hlast_ref, prev_scr, carry_scr, y_scr, *, chain, n_blocks):
    tt, _, c = x_ref.shape
    blk = c // n_blocks
    x = x_ref[...]
    if chain:
        @pl.when(pl.program_id(0) == 0)
        def _():
            prev_scr[...] = jnp.zeros_like(prev_scr)
            carry_scr[...] = jnp.zeros_like(carry_scr)
        xprev = jnp.concatenate([prev_scr[...], x[:-1]], axis=0)
        prev_scr[...] = x[-1:]
    else:
        xprev = prev_ref[...]

    t_idx = lax.broadcasted_iota(jnp.int32, (tt, SUBLANES, blk), 1)
    softplus_neg_lam = jnp.logaddexp(-lam_ref[...], 0.0)
    ysq = jnp.zeros((tt, SUBLANES, 1), F32)
    for k in range(n_blocks):
        cs = slice(k * blk, (k + 1) * blk)
        xk, xpk = x[:, :, cs], xprev[:, :, cs]
        u = cb_ref[:, cs] + cw_ref[CONV_W - 1:CONV_W, cs] * xk
        for s in range(1, CONV_W):
            sh = jnp.where(t_idx >= s, pltpu.roll(xk, s, 1), pltpu.roll(xpk, s, 1))
            u = u + cw_ref[CONV_W - 1 - s:CONV_W - s, cs] * sh
        u2 = u.reshape(tt * SUBLANES, blk)
        gates = _dot(u2.astype(BF16), wax_ref[k])
        r = jax.nn.sigmoid(gates[:, :blk] + ba_ref[:, cs])
        i = jax.nn.sigmoid(gates[:, blk:] + bx_ref[:, cs])
        log_a = (-LRU_C * r) * softplus_neg_lam[:, cs]
        a = jnp.exp(log_a)
        one_minus_a2 = -jnp.tanh(log_a) * (a * a + 1.0)
        b = (jnp.sqrt(one_minus_a2) * (i * u2)).reshape(tt, SUBLANES, blk)
        a = a.reshape(tt, SUBLANES, blk)
        for s in (1, 2, 4):
            m = t_idx >= s
            a_sh = jnp.where(m, pltpu.roll(a, s, 1), 1.0)
            b_sh = jnp.where(m, pltpu.roll(b, s, 1), 0.0)
            b = a * b_sh + b
            a = a * a_sh
        if chain:
            carry = carry_scr[:, cs]
            carries = []
            for j in range(tt):
                carries.append(carry)
                carry = a[j, SUBLANES - 1:, :] * carry + b[j, SUBLANES - 1:, :]
            carry_scr[:, cs] = carry
            hlast_ref[:, cs] = carry
            h_in = jnp.stack(carries, axis=0)
        else:
            h_in = h0_ref[:, :, cs]
        h = a * h_in + b
        if not chain:
            hlast_ref[:, :, cs] = h[:, SUBLANES - 1:, :]
        y = h * _gelu_tanh(gate_ref[:, :, cs])
        ysq = ysq + jnp.sum(y * y, axis=-1, keepdims=True)
        y_scr[:, :, cs] = y
    rinv = lax.rsqrt(ysq * (1.0 / c) + EPS)
    y_ref[...] = (y_scr[...] * rinv * gout_ref[...]).reshape(tt * SUBLANES, c).astype(BF16)


def _rnn(xr, gate, prev8, h0, lw, *, chain, tt):
    n, c = xr.shape
    nt = n // SUBLANES
    assert nt % tt == 0
    n_blocks = lw["w_ax"].shape[0]
    x3 = xr.reshape(nt, SUBLANES, c)
    g3 = gate.reshape(nt, SUBLANES, c)
    tile = pl.BlockSpec((tt, SUBLANES, c), lambda i: (i, 0, 0))
    if chain:
        prev8 = jnp.zeros((1, SUBLANES, c), F32)
        h0 = jnp.zeros((1, 1, c), F32)
        prev_spec, h0_spec = _const_spec((1, SUBLANES, c)), _const_spec((1, 1, c))
        hl_shape = jax.ShapeDtypeStruct((nt // tt, 1, c), F32)
        hl_spec = pl.BlockSpec((None, 1, c), lambda i: (i, 0, 0))
    else:
        prev_spec = tile
        h0_spec = pl.BlockSpec((tt, 1, c), lambda i: (i, 0, 0))
        hl_shape = jax.ShapeDtypeStruct((nt, 1, c), F32)
        hl_spec = pl.BlockSpec((tt, 1, c), lambda i: (i, 0, 0))
    y, hlast = pl.pallas_call(
        functools.partial(_rnn_kernel, chain=chain, n_blocks=n_blocks),
        grid=(nt // tt,),
        in_specs=[tile, tile, prev_spec, h0_spec, _const_spec((CONV_W, c)), _const_spec((1, c)),
                  _const_spec(lw["w_ax"].shape), _const_spec((1, c)), _const_spec((1, c)),
                  _const_spec((1, c)), _const_spec((1, c))],
        out_specs=[pl.BlockSpec((tt * SUBLANES, c), lambda i: (i, 0)), hl_spec],
        out_shape=[jax.ShapeDtypeStruct((n, c), BF16), hl_shape],
        scratch_shapes=[pltpu.VMEM((1, SUBLANES, c), F32), pltpu.VMEM((1, c), F32),
                        pltpu.VMEM((tt, SUBLANES, c), F32)],
        compiler_params=_params(("arbitrary",)),
        name="rnn_chain" if chain else "rnn_tiles",
    )(x3, g3, prev8, h0, lw["conv_w"], lw["conv_b"], lw["w_ax"], lw["lru_ba"], lw["lru_bx"],
      lw["lru_lambda"], lw["g_out_rnn"])
    return y, hlast


def _flash_kernel(q_ref, k_ref, v_ref, o_ref, *, scale, tk):
    tq = q_ref.shape[0]
    qi = pl.program_id(1)
    q = q_ref[...]

    def step(j, carry, masked):
        m, l, acc = carry
        start = pl.multiple_of(j * tk, tk)
        k = k_ref[pl.ds(start, tk), :]
        v = v_ref[pl.ds(start, tk), :]
        s = _dot_nt(q, k) * scale
        if masked:
            rows = lax.broadcasted_iota(jnp.int32, (tq, tk), 0)
            cols = lax.broadcasted_iota(jnp.int32, (tq, tk), 1)
            s = jnp.where(cols <= rows, s, -jnp.inf)
        m_new = jnp.maximum(m, jnp.max(s, axis=-1, keepdims=True))
        p = jnp.exp(s - m_new)
        corr = jnp.exp(m - m_new)
        l = l * corr + jnp.sum(p, axis=-1, keepdims=True)
        acc = acc * corr + _dot(p.astype(BF16), v)
        return m_new, l, acc

    init = (jnp.full((tq, 1), -jnp.inf, F32), jnp.zeros((tq, 1), F32),
            jnp.zeros((tq, v_ref.shape[1]), F32))
    carry = lax.fori_loop(0, qi, lambda j, c: step(j, c, False), init)
    _, l, acc = step(qi, carry, True)
    o_ref[...] = acc / l


def _flash(q, k, v, *, n_heads, scale, tq):
    s = q.shape[0]
    slot = q.shape[1] // n_heads
    vd = v.shape[1] // n_heads
    assert s % tq == 0
    return pl.pallas_call(
        functools.partial(_flash_kernel, scale=scale, tk=tq),
        grid=(n_heads, s // tq),
        in_specs=[pl.BlockSpec((tq, slot), lambda h, i: (i, h)),
                  pl.BlockSpec((s, slot), lambda h, i: (0, h)),
                  pl.BlockSpec((s, vd), lambda h, i: (0, h))],
        out_specs=pl.BlockSpec((tq, vd), lambda h, i: (i, h)),
        out_shape=jax.ShapeDtypeStruct((s, n_heads * vd), F32),
        compiler_params=_params(("parallel", "arbitrary")),
        name="flash_prompt",
    )(q, k, v)


PAGES_PER_BLOCK = 2
N_SLOTS = 8


def _paged_kernel(pt_ref, wukT_ref, qlat_ref, qpe_ref, cnew_ref, kpenew_ref, g1_ref, g2_ref, cos_ref,
                  sin_ref, cosn_ref, sinn_ref, ckv_hbm, kpe_hbm, o_ref, wcat, ckv_buf, kpe_buf, cnew_buf,
                  kpenew_buf, sems, *, layer, scale, qk_dim, page, n_steps):
    n_heads, t_new, rank = qlat_ref.shape
    rope_dim = qpe_ref.shape[2]
    half = rope_dim // 2
    hrows = wukT_ref.shape[0]
    nope = hrows // n_heads
    kb = PAGES_PER_BLOCK * page
    b = pl.program_id(0)
    total = pl.num_programs(0) * n_steps

    def copies(g, slot):
        bg = g // n_steps
        ig = g - bg * n_steps
        out = []
        for p in range(PAGES_PER_BLOCK):
            pid = pt_ref[bg, ig * PAGES_PER_BLOCK + p]
            rows = pl.ds(p * page, page)
            out.append(pltpu.make_async_copy(ckv_hbm.at[layer, pid], ckv_buf.at[slot, rows, :],
                                             sems.at[slot, 2 * p]))
            out.append(pltpu.make_async_copy(kpe_hbm.at[layer, pid], kpe_buf.at[slot, rows, :],
                                             sems.at[slot, 2 * p + 1]))
        return out

    @pl.when(b == 0)
    def _():
        wcat[0:hrows, :] = wukT_ref[...]
        cnew_buf[...] = jnp.zeros_like(cnew_buf)
        kpenew_buf[...] = jnp.zeros_like(kpenew_buf)
        for g in range(N_SLOTS - 1):
            @pl.when(g < total)
            def _():
                for c in copies(g, g):
                    c.start()

    wcat[hrows:, :] = qlat_ref[...].reshape(n_heads * t_new, rank).astype(BF16)
    qpe = qpe_ref[...].reshape(n_heads * t_new, rope_dim).astype(BF16)
    g1 = g1_ref[...]
    g2 = g2_ref[...]

    def block(c32, kpe32, cos, sin, carry, new_block):
        m, l, acc = carry
        nk = c32.shape[0]
        kpe_pad = jnp.concatenate([kpe32, jnp.zeros((nk, LANES - rope_dim), F32)], axis=1)
        c_bf = c32.astype(BF16)
        kt = _dot_nt(wcat[...], c_bf)
        kpe_t = kpe_pad.T[:rope_dim]
        ssq_pe = jnp.sum(kpe_t * kpe_t, axis=0, keepdims=True)
        x1 = kpe_t[:half] * g1[:, :nk]
        x2 = kpe_t[half:] * g2[:, :nk]
        kr = jnp.concatenate([x1 * cos - x2 * sin, x2 * cos + x1 * sin], axis=0).astype(BF16)
        s_raw = kt[hrows:] + _dot(qpe, kr)
        parts = []
        for h in range(n_heads):
            kh = kt[h * nope:(h + 1) * nope]
            ssq = jnp.sum(kh * kh, axis=0, keepdims=True) + ssq_pe
            rinv = lax.rsqrt(ssq * (1.0 / qk_dim) + EPS) * scale
            parts.append(s_raw[h * t_new:(h + 1) * t_new] * rinv)
        s = jnp.concatenate(parts, axis=0)
        if new_block:
            key = lax.broadcasted_iota(jnp.int32, s.shape, 1)
            qt = lax.broadcasted_iota(jnp.int32, s.shape, 0) % t_new
            s = jnp.where(key <= qt, s, -jnp.inf)
        m_new = jnp.maximum(m, jnp.max(s, axis=-1, keepdims=True))
        p = jnp.exp(s - m_new)
        corr = jnp.exp(m - m_new)
        l = l * corr + jnp.sum(p, axis=-1, keepdims=True)
        acc = acc * corr + _dot(p.astype(BF16), c_bf)
        return m_new, l, acc

    def step(i, carry):
        g = b * n_steps + i
        nxt = g + (N_SLOTS - 1)

        @pl.when(nxt < total)
        def _():
            for c in copies(nxt, nxt % N_SLOTS):
                c.start()

        slot = g % N_SLOTS
        for c in copies(g, slot):
            c.wait()
        return block(ckv_buf[slot], kpe_buf[slot], cos_ref[i], sin_ref[i], carry, False)

    rows = n_heads * t_new
    init = (jnp.full((rows, 1), -jnp.inf, F32), jnp.zeros((rows, 1), F32), jnp.zeros((rows, rank), F32))
    carry = lax.fori_loop(0, n_steps, step, init)

    cnew_buf[0:t_new, :] = cnew_ref[...]
    kpenew_buf[0:t_new, :] = kpenew_ref[...]
    _, l, acc = block(cnew_buf[...], kpenew_buf[...], cosn_ref[...], sinn_ref[...], carry, True)
    o_ref[...] = acc / l


def _paged(qlat, qpe, ckv_new, kpe_new, page_table, cache_ckv, cache_kpe, lw, tabs, *, layer, scale):
    n_heads, n, rank = qlat.shape
    rope_dim = qpe.shape[2]
    db, n_pages = page_table.shape
    t_new = n // db
    page = cache_ckv.shape[2]
    assert t_new == SUBLANES and page == LANES and n_pages % PAGES_PER_BLOCK == 0
    n_steps = n_pages // PAGES_PER_BLOCK
    kb = PAGES_PER_BLOCK * page
    half = rope_dim // 2
    hrows = lw["w_ukT"].shape[0]
    rows = n_heads * t_new
    cos_t, sin_t, cos_n, sin_n = tabs
    grid_spec = pltpu.PrefetchScalarGridSpec(
        num_scalar_prefetch=1, grid=(db,),
        in_specs=[pl.BlockSpec((hrows, rank), lambda b, pt: (0, 0)),
                  pl.BlockSpec((n_heads, t_new, rank), lambda b, pt: (0, b, 0)),
                  pl.BlockSpec((n_heads, t_new, rope_dim), lambda b, pt: (0, b, 0)),
                  pl.BlockSpec((t_new, rank), lambda b, pt: (b, 0)),
                  pl.BlockSpec((t_new, rope_dim), lambda b, pt: (b, 0)),
                  pl.BlockSpec((half, kb), lambda b, pt: (0, 0)),
                  pl.BlockSpec((half, kb), lambda b, pt: (0, 0)),
                  pl.BlockSpec((n_steps, half, kb), lambda b, pt: (0, 0, 0)),
                  pl.BlockSpec((n_steps, half, kb), lambda b, pt: (0, 0, 0)),
                  pl.BlockSpec((half, page), lambda b, pt: (0, 0)),
                  pl.BlockSpec((half, page), lambda b, pt: (0, 0)),
                  pl.BlockSpec(memory_space=pl.ANY),
                  pl.BlockSpec(memory_space=pl.ANY)],
        out_specs=pl.BlockSpec((None, rows, rank), lambda b, pt: (b, 0, 0)),
        scratch_shapes=[pltpu.VMEM((hrows + rows, rank), BF16),
                        pltpu.VMEM((N_SLOTS, kb, rank), F32),
                        pltpu.VMEM((N_SLOTS, kb, rope_dim), F32),
                        pltpu.VMEM((page, rank), F32),
                        pltpu.VMEM((page, rope_dim), F32),
                        pltpu.SemaphoreType.DMA((N_SLOTS, 2 * PAGES_PER_BLOCK))])
    return pl.pallas_call(
        functools.partial(_paged_kernel, layer=layer, scale=scale, qk_dim=lw["dims"][4] + rope_dim,
                          page=page, n_steps=n_steps),
        grid_spec=grid_spec,
        out_shape=jax.ShapeDtypeStruct((db, rows, rank), F32),
        compiler_params=_params(("arbitrary",)),
        name="paged_sample",
    )(page_table, lw["w_ukT"], qlat, qpe, ckv_new, kpe_new, lw["gk_pe1"], lw["gk_pe2"], cos_t, sin_t,
      cos_n, sin_n, cache_ckv, cache_kpe)


def _uvproj_kernel(olat_ref, wuv_ref, o_ref):
    db, t_new, rank = olat_ref.shape
    o_ref[...] = _dot(olat_ref[...].reshape(db * t_new, rank).astype(BF16), wuv_ref[...])


def _uvproj(olat, w_uv_h, *, t_new):
    db, rows, rank = olat.shape
    n_heads, _, vd = w_uv_h.shape
    olat4 = olat.reshape(db, n_heads, t_new, rank)
    return pl.pallas_call(
        _uvproj_kernel,
        grid=(n_heads,),
        in_specs=[pl.BlockSpec((db, None, t_new, rank), lambda h: (0, h, 0, 0)),
                  pl.BlockSpec((None, rank, vd), lambda h: (h, 0, 0))],
        out_specs=pl.BlockSpec((db * t_new, vd), lambda h: (0, h)),
        out_shape=jax.ShapeDtypeStruct((db * t_new, n_heads * vd), F32),
        compiler_params=_params(("parallel",)),
        name="uvproj_sample",
    )(olat4, w_uv_h)


def _merge_kernel(x_ref, yr_ref, ya_ref, gattn_ref, wout_ref, gmlp_ref, h_ref, hn_ref):
    d_rnn = yr_ref.shape[1]
    yan = _rms(ya_ref[...], gattn_ref[...]).astype(BF16)
    h = x_ref[...] + _dot(yr_ref[...], wout_ref[0:d_rnn, :]) + _dot(yan, wout_ref[d_rnn:, :])
    h_ref[...] = h
    hn_ref[...] = _rms(h, gmlp_ref[...]).astype(BF16)


def _merge(x, y_rnn, y_attn, lw, *, tm):
    n, d_model = x.shape
    d_rnn, d_attn = y_rnn.shape[1], y_attn.shape[1]
    row = lambda w: pl.BlockSpec((tm, w), lambda i: (i, 0))
    return pl.pallas_call(
        _merge_kernel,
        grid=(n // tm,),
        in_specs=[row(d_model), row(d_rnn), row(d_attn), _const_spec((1, d_attn)),
                  _const_spec(lw["w_out"].shape), _const_spec((1, d_model))],
        out_specs=[row(d_model), row(d_model)],
        out_shape=[jax.ShapeDtypeStruct((n, d_model), F32), jax.ShapeDtypeStruct((n, d_model), BF16)],
        compiler_params=_params(("parallel",)),
        name="merge",
    )(x, y_rnn, y_attn, lw["g_out_attn"], lw["w_out"], lw["g_mlp"])


def _mlp_kernel(h_ref, hn_ref, wup_ref, wdn_ref, o_ref):
    @pl.when(pl.program_id(1) == 0)
    def _():
        o_ref[...] = h_ref[...]
    up = jnp.maximum(_dot(hn_ref[...], wup_ref[...]), 0.0)
    o_ref[...] += _dot((up * up).astype(BF16), wdn_ref[...])


def _mlp(h, hn, lw, *, tm, tf):
    n, d_model = h.shape
    d_ff = lw["w_up"].shape[1]
    assert n % tm == 0 and d_ff % tf == 0
    return pl.pallas_call(
        _mlp_kernel,
        grid=(n // tm, d_ff // tf),
        in_specs=[pl.BlockSpec((tm, d_model), lambda i, f: (i, 0)),
                  pl.BlockSpec((tm, d_model), lambda i, f: (i, 0)),
                  pl.BlockSpec((d_model, tf), lambda i, f: (0, f)),
                  pl.BlockSpec((tf, d_model), lambda i, f: (f, 0))],
        out_specs=pl.BlockSpec((tm, d_model), lambda i, f: (i, 0)),
        out_shape=jax.ShapeDtypeStruct((n, d_model), F32),
        compiler_params=_params(("parallel", "arbitrary")),
        name="mlp",
    )(h, hn, lw["w_up"], lw["w_down"])


def _rope_angles(pos, rope_dim):
    half = rope_dim // 2
    inv_freq = ROPE_THETA ** (-jnp.arange(half, dtype=F32) / half)
    return pos.astype(F32)[:, None] * inv_freq[None, :]


def _token_tables(pos, rope_dim):
    ang = _rope_angles(pos, rope_dim)
    cos, sin = jnp.cos(ang), jnp.sin(ang)
    pad = jnp.zeros((pos.shape[0], LANES - rope_dim), F32)
    return jnp.concatenate([cos, cos, pad], axis=1), jnp.concatenate([-sin, sin, pad], axis=1)


def _key_tables(pos, rope_dim, block):
    ang = _rope_angles(pos, rope_dim).T
    half = rope_dim // 2
    nb = pos.shape[0] // block
    to_blocks = lambda a: a.reshape(half, nb, block).transpose(1, 0, 2)
    return to_blocks(jnp.cos(ang)), to_blocks(jnp.sin(ang))


def _layer_weights(l, w_in, w_qb, w_uk, w_uv, w_out, w_up, w_down, lru_wa, lru_wx, vecs):
    d_model = w_in.shape[1]
    kv_rank, n_heads, nope = w_uk.shape[1:]
    q_rank = w_qb.shape[1]
    d_rnn = lru_wa.shape[1] * lru_wa.shape[2]
    qk_dim = w_qb.shape[2] // n_heads
    rope_dim = qk_dim - nope
    half = rope_dim // 2
    slot = 2 * LANES
    kb = PAGES_PER_BLOCK * LANES
    v = {k: a[l][None, :] for k, a in vecs.items()}
    g_qn, g_kn = v.pop("g_qn"), v.pop("g_kn")
    pad = jnp.zeros((1, LANES - rope_dim), F32)
    lw = dict(v)
    lw["dims"] = (d_rnn, q_rank, kv_rank, n_heads, nope, rope_dim)
    lw["gq0"], lw["gq1"] = g_qn[:, :nope], jnp.concatenate([g_qn[:, nope:], pad], axis=1)
    lw["gk0"], lw["gk1"] = g_kn[:, :nope], jnp.concatenate([g_kn[:, nope:], pad], axis=1)
    lw["gk_pe1"] = jnp.broadcast_to(g_kn[0, nope:nope + half][:, None], (half, kb))
    lw["gk_pe2"] = jnp.broadcast_to(g_kn[0, nope + half:][:, None], (half, kb))
    lw["w_in"] = jnp.concatenate([w_in[l], jnp.zeros((d_model, LANES - rope_dim), F32)], axis=1).astype(BF16)
    wq = w_qb[l].reshape(q_rank, n_heads, qk_dim)
    wq = jnp.concatenate([wq, jnp.zeros((q_rank, n_heads, slot - qk_dim), F32)], axis=2)
    lw["w_qb"] = wq.reshape(q_rank, n_heads * slot).astype(BF16)
    lw["w_uk"] = w_uk[l].reshape(kv_rank, n_heads * nope).astype(BF16)
    lw["w_ukT_h"] = w_uk[l].transpose(1, 2, 0).astype(BF16)
    lw["w_ukT"] = lw["w_ukT_h"].reshape(n_heads * nope, kv_rank)
    lw["w_uv"] = w_uv[l].reshape(kv_rank, -1).astype(BF16)
    lw["w_uv_h"] = w_uv[l].transpose(1, 0, 2).astype(BF16)
    lw["w_ax"] = jnp.concatenate([lru_wa[l], lru_wx[l]], axis=-1).astype(BF16)
    lw["w_out"] = w_out[l].astype(BF16)
    lw["w_up"] = w_up[l].astype(BF16)
    lw["w_down"] = w_down[l].astype(BF16)
    return lw


def _trunk_layer(x_p, x_s, state_conv8, h0_s, page_table, cache_ckv, cache_kpe, lw, tabs, layer, cfg):
    d_rnn, q_rank, kv_rank, n_heads, nope, rope_dim = lw["dims"]
    scale = (nope + rope_dim) ** -0.5
    db = page_table.shape[0]
    t_new = x_s.shape[0] // db

    xr, gate, ckv_p, kpe_p, q, k, v = _prep(x_p, lw, tabs["cos_p"], tabs["sin_p"], prompt=True, tm=cfg["tm"])
    y_rnn, hl_p = _rnn(xr, gate, None, None, lw, chain=True, tt=cfg["tt"])
    y_attn = _flash(q, k, v, n_heads=n_heads, scale=scale, tq=cfg["tq"])
    h, hn = _merge(x_p, y_rnn, y_attn, lw, tm=cfg["tm"])
    y_p = _mlp(h, hn, lw, tm=cfg["tm_mlp"], tf=cfg["tf"])
    conv_p = xr[-(CONV_W - 1):]

    xr, gate, ckv_s, kpe_s, qlat, qpe = _prep(x_s, lw, tabs["cos_s"], tabs["sin_s"], prompt=False,
                                              tm=cfg["tm"])
    y_rnn, hl_s = _rnn(xr, gate, state_conv8, h0_s, lw, chain=False, tt=cfg["tt"])
    olat = _paged(qlat, qpe, ckv_s, kpe_s, page_table, cache_ckv, cache_kpe, lw,
                  (tabs["cos_k"], tabs["sin_k"], tabs["cos_n"], tabs["sin_n"]), layer=layer, scale=scale)
    y_attn = _uvproj(olat, lw["w_uv_h"], t_new=t_new)
    h, hn = _merge(x_s, y_rnn, y_attn, lw, tm=cfg["tm"])
    y_s = _mlp(h, hn, lw, tm=cfg["tm_mlp"], tf=cfg["tf"])
    conv_s = xr.reshape(db, t_new, d_rnn)[:, t_new - (CONV_W - 1):]
    return (y_p, y_s, ckv_p, kpe_p, hl_p[-1], conv_p, ckv_s.reshape(db, t_new, kv_rank),
            kpe_s.reshape(db, t_new, rope_dim), hl_s[:, 0], conv_s)


def kernel(x_prompt, x_sample, cache_ckv, cache_kpe, state_rglru_h, state_conv, page_table, g_mix, w_in,
           conv_w, conv_b, lru_wa, lru_ba, lru_wx, lru_bx, lru_lambda, g_q_a, w_qb, g_kv_a, w_uk, w_uv,
           g_qn, g_kn, g_out_rnn, g_out_attn, w_out, g_mlp, w_up, w_down):
    batch, seq, d_model = x_prompt.shape
    db, t_new, _ = x_sample.shape
    depth = w_in.shape[0]
    assert batch == 1
    n_pages = page_table.shape[1]
    page = cache_ckv.shape[2]
    past = n_pages * page
    rope_dim = cache_kpe.shape[3]
    d_rnn = state_rglru_h.shape[2]
    kb = PAGES_PER_BLOCK * page

    cos_p, sin_p = _token_tables(jnp.arange(seq, dtype=jnp.int32), rope_dim)
    pos_s = past + jnp.arange(t_new, dtype=jnp.int32)
    cos_s, sin_s = _token_tables(jnp.tile(pos_s, db), rope_dim)
    cos_k, sin_k = _key_tables(jnp.arange(past, dtype=jnp.int32), rope_dim, kb)
    cos_n, sin_n = _key_tables(past + jnp.arange(page, dtype=jnp.int32), rope_dim, page)
    tabs = dict(cos_p=cos_p, sin_p=sin_p, cos_s=cos_s, sin_s=sin_s, cos_k=cos_k, sin_k=sin_k,
                cos_n=cos_n[0], sin_n=sin_n[0])
    cfg = dict(tm=min(512, seq, db * t_new), tt=min(32, db * t_new // SUBLANES), tq=min(512, seq),
               tm_mlp=min(512, seq, db * t_new), tf=min(1024, w_up.shape[2]))

    vecs = dict(g_mix=g_mix, conv_b=conv_b, lru_ba=lru_ba.reshape(depth, d_rnn),
                lru_bx=lru_bx.reshape(depth, d_rnn), lru_lambda=lru_lambda, g_q_a=g_q_a, g_kv_a=g_kv_a,
                g_qn=g_qn, g_kn=g_kn, g_out_rnn=g_out_rnn, g_out_attn=g_out_attn, g_mlp=g_mlp)
    y_p = x_prompt.reshape(seq, d_model)
    y_s = x_sample.reshape(db * t_new, d_model)
    state_conv8 = jnp.pad(state_conv, ((0, 0), (0, 0), (SUBLANES - (CONV_W - 1), 0), (0, 0)))
    outs = []
    for l in range(depth):
        lw = _layer_weights(l, w_in, w_qb, w_uk, w_uv, w_out, w_up, w_down, lru_wa, lru_wx, vecs)
        lw["conv_w"] = conv_w[l]
        res = _trunk_layer(y_p, y_s, state_conv8[l], state_rglru_h[l][:, None, :], page_table, cache_ckv,
                           cache_kpe, lw, tabs, l, cfg)
        y_p, y_s = res[0], res[1]
        outs.append(res[2:])
    ckv_p, kpe_p, h_p, conv_p, ckv_s, kpe_s, h_s, conv_s = (jnp.stack(a) for a in zip(*outs))
    return (y_p.reshape(batch, seq, d_model), y_s.reshape(db, t_new, d_model),
            ckv_p[:, None], kpe_p[:, None], h_p, conv_p[:, None], ckv_s, kpe_s, h_s, conv_s)
```

```python
import functools
import math

import jax
import jax.numpy as jnp
from jax import lax
from jax.experimental import pallas as pl
from jax.experimental.pallas import tpu as pltpu

EPS = 1e-6
ROPE_THETA = 10000.0
LRU_C = 8.0
CONV_W = 4
LOG2_E = math.log2(math.e)

SUBLANES = 8
LANES = 128
MXU_DIM = 256
VMEM_LIMIT_BYTES = 56 * 1024 * 1024

BF16 = jnp.bfloat16
F32 = jnp.float32

_NT_DIMS = (((1,), (1,)), ((), ()))


def _dot(a, b):
    return jnp.dot(a, b, preferred_element_type=F32)


def _dot_nt(a, b):
    return lax.dot_general(a, b, _NT_DIMS, preferred_element_type=F32)


def _rms(x, g):
    return x * lax.rsqrt(jnp.mean(x * x, axis=-1, keepdims=True) + EPS) * g


def _const_spec(shape):
    zeros = (0,) * len(shape)
    return pl.BlockSpec(shape, lambda *_: zeros)


def _params(semantics):
    return pltpu.CompilerParams(dimension_semantics=semantics, vmem_limit_bytes=VMEM_LIMIT_BYTES)


def _head_slot(v0, v1, g0, g1, cos, sin, qk_dim):
    ssq = jnp.sum(v0 * v0, axis=-1, keepdims=True) + jnp.sum(v1 * v1, axis=-1, keepdims=True)
    rinv = lax.rsqrt(ssq * (1.0 / qk_dim) + EPS)
    n0 = v0 * rinv * g0
    n1 = v1 * rinv * g1
    return n0, _rope_slot(n1, cos, sin)


def _rope_slot(n1, cos, sin):
    partner = pltpu.roll(n1, 32, 1) + pltpu.roll(n1, 96, 1)
    return n1 * cos + partner * sin


def _prep_kernel(x_ref, gmix_ref, win_ref, gqa_ref, wqb_ref, gkva_ref, gq0_ref, gq1_ref, gk0_ref,
                 gk1_ref, cos_ref, sin_ref, wk_ref, wv_ref, *out_refs, dims, prompt):
    d_rnn, q_rank, kv_rank, n_heads, nope, rope_dim = dims
    qk_dim = nope + rope_dim
    qscale = qk_dim ** -0.5 * LOG2_E
    slot = 2 * LANES
    if prompt:
        xr_ref, gate_ref, ckv_ref, kpe_ref, q_ref, k_ref, v_ref = out_refs
    else:
        xr_ref, gate_ref, ckv_ref, kpe_ref, qlat_ref, qpe_ref = out_refs

    xn = _rms(x_ref[...], gmix_ref[...]).astype(BF16)
    o = 0
    xr_ref[...] = _dot(xn, win_ref[:, o:o + d_rnn]); o += d_rnn
    gate_ref[...] = _dot(xn, win_ref[:, o:o + d_rnn]); o += d_rnn
    qa = _dot(xn, win_ref[:, o:o + q_rank]); o += q_rank
    kvc = _dot(xn, win_ref[:, o:o + kv_rank]); o += kv_rank
    kpe = _dot(xn, win_ref[:, o:o + LANES])

    cos = cos_ref[...]
    sin = sin_ref[...]
    ckv = _rms(kvc, gkva_ref[...])
    ckv_ref[...] = ckv
    kpe_ref[...] = kpe[:, :rope_dim]

    qf = _dot(_rms(qa, gqa_ref[...]).astype(BF16), wqb_ref[...])
    gq0, gq1 = gq0_ref[...], gq1_ref[...]
    gk0, gk1 = gk0_ref[...], gk1_ref[...]
    for h in range(n_heads):
        v0 = qf[:, h * slot:h * slot + LANES]
        v1 = qf[:, h * slot + LANES:(h + 1) * slot]
        n0, r1 = _head_slot(v0, v1, gq0, gq1, cos, sin, qk_dim)
        if prompt:
            q_ref[:, h * slot:h * slot + LANES] = (n0 * qscale).astype(BF16)
            q_ref[:, h * slot + LANES:(h + 1) * slot] = (r1 * qscale).astype(BF16)
        else:
            qlat_ref[h] = _dot((n0 * gk0).astype(BF16), wk_ref[h])
            qpe_ref[h] = r1[:, :rope_dim]

    if prompt:
        ckv_bf = ckv.astype(BF16)
        v_ref[...] = _dot(ckv_bf, wv_ref[...]).astype(BF16)
        kn = _dot(ckv_bf, wk_ref[...])
        ssq_pe = jnp.sum(kpe * kpe, axis=-1, keepdims=True)
        kr = _rope_slot(kpe * gk1, cos, sin)
        for h in range(n_heads):
            kh = kn[:, h * nope:(h + 1) * nope]
            ssq = jnp.sum(kh * kh, axis=-1, keepdims=True) + ssq_pe
            rinv = lax.rsqrt(ssq * (1.0 / qk_dim) + EPS)
            k_ref[:, h * slot:h * slot + LANES] = (kh * rinv * gk0).astype(BF16)
            k_ref[:, h * slot + LANES:(h + 1) * slot] = (kr * rinv).astype(BF16)


def _prep(x, lw, cos, sin, *, prompt, tm):
    n, d_model = x.shape
    dims = lw["dims"]
    d_rnn, q_rank, kv_rank, n_heads, nope, rope_dim = dims
    slot = 2 * LANES
    assert nope == LANES and rope_dim == LANES // 2 and n % tm == 0
    row = lambda w: pl.BlockSpec((tm, w), lambda i: (i, 0))
    wk = lw["w_uk"] if prompt else lw["w_ukT_h"]
    in_specs = [row(d_model), _const_spec((1, d_model)), _const_spec(lw["w_in"].shape),
                _const_spec((1, q_rank)), _const_spec(lw["w_qb"].shape), _const_spec((1, kv_rank)),
                _const_spec((1, LANES)), _const_spec((1, LANES)), _const_spec((1, LANES)),
                _const_spec((1, LANES)), row(LANES), row(LANES), _const_spec(wk.shape),
                _const_spec(lw["w_uv"].shape)]
    out_shape = [jax.ShapeDtypeStruct((n, d_rnn), F32), jax.ShapeDtypeStruct((n, d_rnn), F32),
                 jax.ShapeDtypeStruct((n, kv_rank), F32), jax.ShapeDtypeStruct((n, rope_dim), F32)]
    out_specs = [row(d_rnn), row(d_rnn), row(kv_rank), row(rope_dim)]
    if prompt:
        out_shape += [jax.ShapeDtypeStruct((n, n_heads * slot), BF16),
                      jax.ShapeDtypeStruct((n, n_heads * slot), BF16),
                      jax.ShapeDtypeStruct((n, n_heads * nope), BF16)]
        out_specs += [row(n_heads * slot), row(n_heads * slot), row(n_heads * nope)]
    else:
        out_shape += [jax.ShapeDtypeStruct((n_heads, n, kv_rank), F32),
                      jax.ShapeDtypeStruct((n_heads, n, rope_dim), F32)]
        out_specs += [pl.BlockSpec((n_heads, tm, kv_rank), lambda i: (0, i, 0)),
                      pl.BlockSpec((n_heads, tm, rope_dim), lambda i: (0, i, 0))]
    return pl.pallas_call(
        functools.partial(_prep_kernel, dims=dims, prompt=prompt),
        grid=(n // tm,), in_specs=in_specs, out_specs=out_specs, out_shape=out_shape,
        compiler_params=_params(("parallel",)),
        name="prep_prompt" if prompt else "prep_sample",
    )(x, lw["g_mix"], lw["w_in"], lw["g_q_a"], lw["w_qb"], lw["g_kv_a"], lw["gq0"], lw["gq1"],
      lw["gk0"], lw["gk1"], cos, sin, wk, lw["w_uv"])


def _gelu_tanh(x):
    c = math.sqrt(2.0 / math.pi)
    return 0.5 * x * (1.0 + jnp.tanh(c * (x + 0.044715 * (x * x * x))))


def _rnn_kernel(x_ref, gate_ref, prev_ref, h0_ref, cw_ref, cb_ref, wax_ref, ba_ref, bx_ref, lam_ref,
                gout_ref, y_ref, hlast_ref, prev_scr, carry_scr, y_scr, *, chain, n_blocks):
    tt, _, c = x_ref.shape
    blk = c // n_blocks
    x = x_ref[...]
    if chain:
        @pl.when(pl.program_id(0) == 0)
        def _():
            prev_scr[...] = jnp.zeros_like(prev_scr)
            carry_scr[...] = jnp.zeros_like(carry_scr)
        xprev = jnp.concatenate([prev_scr[...], x[:-1]], axis=0)
        prev_scr[...] = x[-1:]
    else:
        xprev = prev_ref[...]

    t_idx = lax.broadcasted_iota(jnp.int32, (tt, SUBLANES, blk), 1)
    softplus_neg_lam = jnp.logaddexp(-lam_ref[...], 0.0)
    ysq = jnp.zeros((tt, SUBLANES, 1), F32)
    for k in range(n_blocks):
        cs = slice(k * blk, (k + 1) * blk)
        xk, xpk = x[:, :, cs], xprev[:, :, cs]
        u = cb_ref[:, cs] + cw_ref[CONV_W - 1:CONV_W, cs] * xk
        for s in range(1, CONV_W):
            sh = jnp.where(t_idx >= s, pltpu.roll(xk, s, 1), pltpu.roll(xpk, s, 1))
            u = u + cw_ref[CONV_W - 1 - s:CONV_W - s, cs] * sh
        u2 = u.reshape(tt * SUBLANES, blk)
        gates = _dot(u2.astype(BF16), wax_ref[k])
        r = jax.nn.sigmoid(gates[:, :blk] + ba_ref[:, cs])
        i = jax.nn.sigmoid(gates[:, blk:] + bx_ref[:, cs])
        log_a = (-LRU_C * r) * softplus_neg_lam[:, cs]
        a = jnp.exp(log_a)
        one_minus_a2 = -jnp.tanh(log_a) * (a * a + 1.0)
        b = (jnp.sqrt(one_minus_a2) * (i * u2)).reshape(tt, SUBLANES, blk)
        a = a.reshape(tt, SUBLANES, blk)
        for s in (1, 2, 4):
            m = t_idx >= s
            a_sh = jnp.where(m, pltpu.roll(a, s, 1), 1.0)
            b_sh = jnp.where(m, pltpu.roll(b, s, 1), 0.0)
            b = a * b_sh + b
            a = a * a_sh
        if chain:
            carry = carry_scr[:, cs]
            carries = []
            for j in range(tt):
                carries.append(carry)
                carry = a[j, SUBLANES - 1:, :] * carry + b[j, SUBLANES - 1:, :]
            carry_scr[:, cs] = carry
            hlast_ref[:, cs] = carry
            h_in = jnp.stack(carries, axis=0)
        else:
            h_in = h0_ref[:, :, cs]
        h = a * h_in + b
        if not chain:
            hlast_ref[:, :, cs] = h[:, SUBLANES - 1:, :]
        y = h * _gelu_tanh(gate_ref[:, :, cs])
        ysq = ysq + jnp.sum(y * y, axis=-1, keepdims=True)
        y_scr[:, :, cs] = y
    rinv = lax.rsqrt(ysq * (1.0 / c) + EPS)
    y_ref[...] = (y_scr[...] * rinv * gout_ref[...]).reshape(tt * SUBLANES, c).astype(BF16)


def _rnn(xr, gate, prev8, h0, lw, *, chain, tt):
    n, c = xr.shape
    nt = n // SUBLANES
    assert nt % tt == 0
    n_blocks = lw["w_ax"].shape[0]
    x3 = xr.reshape(nt, SUBLANES, c)
    g3 = gate.reshape(nt, SUBLANES, c)
    tile = pl.BlockSpec((tt, SUBLANES, c), lambda i: (i, 0, 0))
    if chain:
        prev8 = jnp.zeros((1, SUBLANES, c), F32)
        h0 = jnp.zeros((1, 1, c), F32)
        prev_spec, h0_spec = _const_spec((1, SUBLANES, c)), _const_spec((1, 1, c))
        hl_shape = jax.ShapeDtypeStruct((nt // tt, 1, c), F32)
        hl_spec = pl.BlockSpec((None, 1, c), lambda i: (i, 0, 0))
    else:
        prev_spec = tile
        h0_spec = pl.BlockSpec((tt, 1, c), lambda i: (i, 0, 0))
        hl_shape = jax.ShapeDtypeStruct((nt, 1, c), F32)
        hl_spec = pl.BlockSpec((tt, 1, c), lambda i: (i, 0, 0))
    y, hlast = pl.pallas_call(
        functools.partial(_rnn_kernel, chain=chain, n_blocks=n_blocks),
        grid=(nt // tt,),
        in_specs=[tile, tile, prev_spec, h0_spec, _const_spec((CONV_W, c)), _const_spec((1, c)),
                  _const_spec(lw["w_ax"].shape), _const_spec((1, c)), _const_spec((1, c)),
                  _const_spec((1, c)), _const_spec((1, c))],
        out_specs=[pl.BlockSpec((tt * SUBLANES, c), lambda i: (i, 0)), hl_spec],
        out_shape=[jax.ShapeDtypeStruct((n, c), BF16), hl_shape],
        scratch_shapes=[pltpu.VMEM((1, SUBLANES, c), F32), pltpu.VMEM((1, c), F32),
                        pltpu.VMEM((tt, SUBLANES, c), F32)],
        compiler_params=_params(("arbitrary",)),
        name="rnn_chain" if chain else "rnn_tiles",
    )(x3, g3, prev8, h0, lw["conv_w"], lw["conv_b"], lw["w_ax"], lw["lru_ba"], lw["lru_bx"],
      lw["lru_lambda"], lw["g_out_rnn"])
    return y, hlast


def _lane_partial_sum(p):
    out = p[:, 0:LANES]
    for c in range(1, p.shape[1] // LANES):
        out = out + p[:, c * LANES:(c + 1) * LANES]
    return out


def _softmax_update(s, values, carry):
    m, l, acc = carry
    m_new = jnp.maximum(m, jnp.max(s, axis=-1, keepdims=True))
    p = jnp.exp2(s - m_new)
    corr = jnp.exp2(m - m_new)
    l = l * corr + _lane_partial_sum(p)
    acc = acc * corr + _dot(p.astype(BF16), values)
    return m_new, l, acc


def _softmax_init(rows, width):
    return (jnp.full((rows, 1), -jnp.inf, F32), jnp.zeros((rows, LANES), F32), jnp.zeros((rows, width), F32))


def _softmax_finish(carry):
    _, l, acc = carry
    return acc / jnp.sum(l, axis=-1, keepdims=True)


FLASH_ROW_CHUNK = 64


def _flash_kernel(q_ref, k_ref, v_ref, o_ref, s_scr, p_scr, corr_scr, m_scr, l_scr, acc_scr, *, tk):
    tq = q_ref.shape[0]
    assert tq == 2 * tk
    qi = pl.program_id(1)

    def rows_of(j):
        return pl.ds(pl.multiple_of(j * tk, tk), tk)

    def scores(j, slot):
        s_scr[slot] = _dot_nt(q_ref[...], k_ref[rows_of(j), :])

    def softmax(slot, col_offset=None):
        for r in range(0, tq, FLASH_ROW_CHUNK):
            rs = slice(r, r + FLASH_ROW_CHUNK)
            s = s_scr[slot, rs, :]
            if col_offset is not None:
                row = r + lax.broadcasted_iota(jnp.int32, s.shape, 0)
                col = col_offset + lax.broadcasted_iota(jnp.int32, s.shape, 1)
                s = jnp.where(col <= row, s, -jnp.inf)
            m = m_scr[rs, :]
            m_new = jnp.maximum(m, jnp.max(s, axis=-1, keepdims=True))
            corr = jnp.exp2(m - m_new)
            m_scr[rs, :] = m_new
            corr_scr[slot, rs, :] = corr
            l = l_scr[rs, :] * corr
            for c in range(0, tk, LANES):
                p = jnp.exp2(s[:, c:c + LANES] - m_new)
                l = l + p
                p_scr[slot, rs, c:c + LANES] = p.astype(BF16)
            l_scr[rs, :] = l

    def values(j, slot):
        v = v_ref[rows_of(jnp.maximum(j, 0)), :]
        acc_scr[...] = acc_scr[...] * corr_scr[slot] + _dot(p_scr[slot], v)

    m_scr[...] = jnp.full(m_scr.shape, -jnp.inf, F32)
    l_scr[...] = jnp.zeros(l_scr.shape, F32)
    acc_scr[...] = jnp.zeros(acc_scr.shape, F32)
    p_scr[1] = jnp.zeros(p_scr.shape[1:], BF16)
    corr_scr[1] = jnp.ones(corr_scr.shape[1:], F32)
    scores(0, 0)

    def body(t, _):
        j = 2 * t
        scores(j + 1, 1)
        softmax(0)
        values(j - 1, 1)
        scores(j + 2, 0)
        softmax(1)
        values(j, 0)
        return 0

    lax.fori_loop(0, qi, body, 0)
    j = 2 * qi
    scores(j + 1, 1)
    softmax(0, col_offset=0)
    values(j - 1, 1)
    softmax(1, col_offset=tk)
    values(j, 0)
    values(j + 1, 1)
    o_ref[...] = acc_scr[...] / jnp.sum(l_scr[...], axis=-1, keepdims=True)


def _flash(q, k, v, *, n_heads, tq):
    s = q.shape[0]
    slot = q.shape[1] // n_heads
    vd = v.shape[1] // n_heads
    assert s % tq == 0
    tk = tq // 2
    return pl.pallas_call(
        functools.partial(_flash_kernel, tk=tk),
        grid=(n_heads, s // tq),
        in_specs=[pl.BlockSpec((tq, slot), lambda h, i: (i, h)),
                  pl.BlockSpec((s, slot), lambda h, i: (0, h)),
                  pl.BlockSpec((s, vd), lambda h, i: (0, h))],
        out_specs=pl.BlockSpec((tq, vd), lambda h, i: (i, h)),
        out_shape=jax.ShapeDtypeStruct((s, n_heads * vd), F32),
        scratch_shapes=[pltpu.VMEM((2, tq, tk), F32), pltpu.VMEM((2, tq, tk), BF16),
                        pltpu.VMEM((2, tq, LANES), F32), pltpu.VMEM((tq, LANES), F32),
                        pltpu.VMEM((tq, LANES), F32), pltpu.VMEM((tq, vd), F32)],
        compiler_params=_params(("parallel", "arbitrary")),
        name="flash_prompt",
    )(q, k, v)


PAGES_PER_BLOCK = 8
N_SLOTS = 5


def _paged_kernel(pt_ref, wukT_ref, qlat_ref, qpe_ref, cnew_ref, kpenew_ref, g1_ref, g2_ref, cos_ref,
                  sin_ref, cosn_ref, sinn_ref, ckv_hbm, kpe_hbm, o_ref, wcat, ckv_buf, kpe_buf, s_scr, sems,
                  *, layer, scale, qk_dim, page, n_steps):
    n_heads, t_new, rank = qlat_ref.shape
    rope_dim = qpe_ref.shape[2]
    half = rope_dim // 2
    hrows = wukT_ref.shape[0]
    nope = hrows // n_heads
    b = pl.program_id(0)
    total = pl.num_programs(0) * n_steps

    def copies(g, slot):
        bg = g // n_steps
        ig = g - bg * n_steps
        out = []
        for p in range(PAGES_PER_BLOCK):
            pid = pt_ref[bg, ig * PAGES_PER_BLOCK + p]
            out.append(pltpu.make_async_copy(ckv_hbm.at[layer, pid],
                                             ckv_buf.at[slot, pl.ds(p * page, page), :],
                                             sems.at[slot, 2 * p]))
            out.append(pltpu.make_async_copy(kpe_hbm.at[layer, pid], kpe_buf.at[slot, p],
                                             sems.at[slot, 2 * p + 1]))
        return out

    def issue(g):
        nxt = g + (N_SLOTS - 1)

        @pl.when(nxt < total)
        def _():
            for c in copies(nxt, nxt % N_SLOTS):
                c.start()

    def wait(g):
        for c in copies(g, g % N_SLOTS):
            c.wait()

    g0 = b * n_steps

    @pl.when(b == 0)
    def _():
        wcat[0:hrows, :] = wukT_ref[...]
        for g in range(N_SLOTS - 1):
            @pl.when(g < total)
            def _():
                for c in copies(g, g):
                    c.start()

    wcat[hrows:, :] = qlat_ref[...].reshape(n_heads * t_new, rank).astype(BF16)
    qpe = qpe_ref[...].reshape(n_heads * t_new, rope_dim).astype(BF16)

    def scores(c_bf, kpe_t, g1, g2, cos, sin):
        kt = _dot_nt(wcat[...], c_bf)
        ssq_pe = jnp.sum(kpe_t * kpe_t, axis=0, keepdims=True)
        x1 = kpe_t[:half] * g1
        x2 = kpe_t[half:] * g2
        kr = jnp.concatenate([x1 * cos - x2 * sin, x2 * cos + x1 * sin], axis=0).astype(BF16)
        s_raw = kt[hrows:] + _dot(qpe, kr)
        parts = []
        for h in range(n_heads):
            kh = kt[h * nope:(h + 1) * nope]
            ssq = jnp.sum(kh * kh, axis=0, keepdims=True) + ssq_pe
            rinv = lax.rsqrt(ssq * (1.0 / qk_dim) + EPS) * (scale * LOG2_E)
            parts.append(s_raw[h * t_new:(h + 1) * t_new] * rinv)
        return jnp.concatenate(parts, axis=0)

    def page_scores(g, i):
        slot = g % N_SLOTS
        kpe_t = jnp.concatenate([kpe_buf[slot, p] for p in range(PAGES_PER_BLOCK)], axis=1)
        return scores(ckv_buf[slot].astype(BF16), kpe_t, g1_ref[...], g2_ref[...], cos_ref[i], sin_ref[i])

    def page_update(g, s, carry):
        return _softmax_update(s, ckv_buf[g % N_SLOTS].astype(BF16), carry)

    wait(g0)
    s_scr[...] = page_scores(g0, 0)

    def step(i, carry):
        g = g0 + i
        issue(g)
        wait(g + 1)
        s = s_scr[...]
        s_scr[...] = page_scores(g + 1, i + 1)
        return page_update(g, s, carry)

    rows = n_heads * t_new
    carry = lax.fori_loop(0, n_steps - 1, step, _softmax_init(rows, rank))
    g_last = g0 + n_steps - 1
    issue(g_last)
    carry = page_update(g_last, s_scr[...], carry)

    c_new = jnp.concatenate([cnew_ref[...], jnp.zeros((page - t_new, rank), F32)], axis=0).astype(BF16)
    s = scores(c_new, kpenew_ref[...], g1_ref[:, 0:page], g2_ref[:, 0:page], cosn_ref[...], sinn_ref[...])
    key = lax.broadcasted_iota(jnp.int32, s.shape, 1)
    qt = lax.broadcasted_iota(jnp.int32, s.shape, 0) % t_new
    s = jnp.where(key <= qt, s, -jnp.inf)
    o_ref[...] = _softmax_finish(_softmax_update(s, c_new, carry))


def _paged(qlat, qpe, ckv_new, kpe_new, page_table, cache_ckv, cache_kpe_t, lw, tabs, *, layer, scale):
    n_heads, n, rank = qlat.shape
    rope_dim = qpe.shape[2]
    db, n_pages = page_table.shape
    t_new = n // db
    page = cache_ckv.shape[2]
    assert t_new == SUBLANES and page == LANES and n_pages % PAGES_PER_BLOCK == 0
    n_steps = n_pages // PAGES_PER_BLOCK
    kb = PAGES_PER_BLOCK * page
    half = rope_dim // 2
    hrows = lw["w_ukT"].shape[0]
    rows = n_heads * t_new
    cos_t, sin_t, cos_n, sin_n = tabs
    kpe_new_t = jnp.pad(kpe_new.reshape(db, t_new, rope_dim).transpose(0, 2, 1),
                        ((0, 0), (0, 0), (0, page - t_new)))
    grid_spec = pltpu.PrefetchScalarGridSpec(
        num_scalar_prefetch=1, grid=(db,),
        in_specs=[pl.BlockSpec((hrows, rank), lambda b, pt: (0, 0)),
                  pl.BlockSpec((n_heads, t_new, rank), lambda b, pt: (0, b, 0)),
                  pl.BlockSpec((n_heads, t_new, rope_dim), lambda b, pt: (0, b, 0)),
                  pl.BlockSpec((t_new, rank), lambda b, pt: (b, 0)),
                  pl.BlockSpec((None, rope_dim, page), lambda b, pt: (b, 0, 0)),
                  pl.BlockSpec((half, kb), lambda b, pt: (0, 0)),
                  pl.BlockSpec((half, kb), lambda b, pt: (0, 0)),
                  pl.BlockSpec((n_steps, half, kb), lambda b, pt: (0, 0, 0)),
                  pl.BlockSpec((n_steps, half, kb), lambda b, pt: (0, 0, 0)),
                  pl.BlockSpec((half, page), lambda b, pt: (0, 0)),
                  pl.BlockSpec((half, page), lambda b, pt: (0, 0)),
                  pl.BlockSpec(memory_space=pl.ANY),
                  pl.BlockSpec(memory_space=pl.ANY)],
        out_specs=pl.BlockSpec((None, rows, rank), lambda b, pt: (b, 0, 0)),
        scratch_shapes=[pltpu.VMEM((hrows + rows, rank), BF16),
                        pltpu.VMEM((N_SLOTS, kb, rank), F32),
                        pltpu.VMEM((N_SLOTS, PAGES_PER_BLOCK, rope_dim, page), F32),
                        pltpu.VMEM((rows, kb), F32),
                        pltpu.SemaphoreType.DMA((N_SLOTS, 2 * PAGES_PER_BLOCK))])
    return pl.pallas_call(
        functools.partial(_paged_kernel, layer=layer, scale=scale, qk_dim=lw["dims"][4] + rope_dim,
                          page=page, n_steps=n_steps),
        grid_spec=grid_spec,
        out_shape=jax.ShapeDtypeStruct((db, rows, rank), F32),
        compiler_params=_params(("arbitrary",)),
        name="paged_sample",
    )(page_table, lw["w_ukT"], qlat, qpe, ckv_new, kpe_new_t, lw["gk_pe1"], lw["gk_pe2"], cos_t, sin_t,
      cos_n, sin_n, cache_ckv, cache_kpe_t)


def _uvproj_kernel(olat_ref, wuv_ref, o_ref):
    db, t_new, rank = olat_ref.shape
    o_ref[...] = _dot(olat_ref[...].reshape(db * t_new, rank).astype(BF16), wuv_ref[...])


def _uvproj(olat, w_uv_h, *, t_new):
    db, rows, rank = olat.shape
    n_heads, _, vd = w_uv_h.shape
    olat4 = olat.reshape(db, n_heads, t_new, rank)
    return pl.pallas_call(
        _uvproj_kernel,
        grid=(n_heads,),
        in_specs=[pl.BlockSpec((db, None, t_new, rank), lambda h: (0, h, 0, 0)),
                  pl.BlockSpec((None, rank, vd), lambda h: (h, 0, 0))],
        out_specs=pl.BlockSpec((db * t_new, vd), lambda h: (0, h)),
        out_shape=jax.ShapeDtypeStruct((db * t_new, n_heads * vd), F32),
        compiler_params=_params(("parallel",)),
        name="uvproj_sample",
    )(olat4, w_uv_h)


def _merge_kernel(x_ref, yr_ref, ya_ref, gattn_ref, wout_ref, gmlp_ref, h_ref, hn_ref):
    d_rnn = yr_ref.shape[1]
    yan = _rms(ya_ref[...], gattn_ref[...]).astype(BF16)
    h = x_ref[...] + _dot(yr_ref[...], wout_ref[0:d_rnn, :]) + _dot(yan, wout_ref[d_rnn:, :])
    h_ref[...] = h
    hn_ref[...] = _rms(h, gmlp_ref[...]).astype(BF16)


def _merge(x, y_rnn, y_attn, lw, *, tm):
    n, d_model = x.shape
    d_rnn, d_attn = y_rnn.shape[1], y_attn.shape[1]
    row = lambda w: pl.BlockSpec((tm, w), lambda i: (i, 0))
    return pl.pallas_call(
        _merge_kernel,
        grid=(n // tm,),
        in_specs=[row(d_model), row(d_rnn), row(d_attn), _const_spec((1, d_attn)),
                  _const_spec(lw["w_out"].shape), _const_spec((1, d_model))],
        out_specs=[row(d_model), row(d_model)],
        out_shape=[jax.ShapeDtypeStruct((n, d_model), F32), jax.ShapeDtypeStruct((n, d_model), BF16)],
        compiler_params=_params(("parallel",)),
        name="merge",
    )(x, y_rnn, y_attn, lw["g_out_attn"], lw["w_out"], lw["g_mlp"])


def _mlp_kernel(h_ref, hn_ref, wup_ref, wdn_ref, o_ref):
    @pl.when(pl.program_id(1) == 0)
    def _():
        o_ref[...] = h_ref[...]
    up = jnp.maximum(_dot(hn_ref[...], wup_ref[...]), 0.0)
    o_ref[...] += _dot((up * up).astype(BF16), wdn_ref[...])


def _mlp(h, hn, lw, *, tm, tf):
    n, d_model = h.shape
    d_ff = lw["w_up"].shape[1]
    assert n % tm == 0 and d_ff % tf == 0
    return pl.pallas_call(
        _mlp_kernel,
        grid=(n // tm, d_ff // tf),
        in_specs=[pl.BlockSpec((tm, d_model), lambda i, f: (i, 0)),
                  pl.BlockSpec((tm, d_model), lambda i, f: (i, 0)),
                  pl.BlockSpec((d_model, tf), lambda i, f: (0, f)),
                  pl.BlockSpec((tf, d_model), lambda i, f: (f, 0))],
        out_specs=pl.BlockSpec((tm, d_model), lambda i, f: (i, 0)),
        out_shape=jax.ShapeDtypeStruct((n, d_model), F32),
        compiler_params=_params(("parallel", "arbitrary")),
        name="mlp",
    )(h, hn, lw["w_up"], lw["w_down"])


def _rope_angles(pos, rope_dim):
    half = rope_dim // 2
    inv_freq = ROPE_THETA ** (-jnp.arange(half, dtype=F32) / half)
    return pos.astype(F32)[:, None] * inv_freq[None, :]


def _token_tables(pos, rope_dim):
    ang = _rope_angles(pos, rope_dim)
    cos, sin = jnp.cos(ang), jnp.sin(ang)
    pad = jnp.zeros((pos.shape[0], LANES - rope_dim), F32)
    return jnp.concatenate([cos, cos, pad], axis=1), jnp.concatenate([-sin, sin, pad], axis=1)


def _key_tables(pos, rope_dim, block):
    ang = _rope_angles(pos, rope_dim).T
    half = rope_dim // 2
    nb = pos.shape[0] // block
    to_blocks = lambda a: a.reshape(half, nb, block).transpose(1, 0, 2)
    return to_blocks(jnp.cos(ang)), to_blocks(jnp.sin(ang))


def _layer_weights(l, w_in, w_qb, w_uk, w_uv, w_out, w_up, w_down, lru_wa, lru_wx, vecs):
    d_model = w_in.shape[1]
    kv_rank, n_heads, nope = w_uk.shape[1:]
    q_rank = w_qb.shape[1]
    d_rnn = lru_wa.shape[1] * lru_wa.shape[2]
    qk_dim = w_qb.shape[2] // n_heads
    rope_dim = qk_dim - nope
    half = rope_dim // 2
    slot = 2 * LANES
    kb = PAGES_PER_BLOCK * LANES
    v = {k: a[l][None, :] for k, a in vecs.items()}
    g_qn, g_kn = v.pop("g_qn"), v.pop("g_kn")
    pad = jnp.zeros((1, LANES - rope_dim), F32)
    lw = dict(v)
    lw["dims"] = (d_rnn, q_rank, kv_rank, n_heads, nope, rope_dim)
    lw["gq0"], lw["gq1"] = g_qn[:, :nope], jnp.concatenate([g_qn[:, nope:], pad], axis=1)
    lw["gk0"], lw["gk1"] = g_kn[:, :nope], jnp.concatenate([g_kn[:, nope:], pad], axis=1)
    lw["gk_pe1"] = jnp.broadcast_to(g_kn[0, nope:nope + half][:, None], (half, kb))
    lw["gk_pe2"] = jnp.broadcast_to(g_kn[0, nope + half:][:, None], (half, kb))
    lw["w_in"] = jnp.concatenate([w_in[l], jnp.zeros((d_model, LANES - rope_dim), F32)], axis=1).astype(BF16)
    wq = w_qb[l].reshape(q_rank, n_heads, qk_dim)
    wq = jnp.concatenate([wq, jnp.zeros((q_rank, n_heads, slot - qk_dim), F32)], axis=2)
    lw["w_qb"] = wq.reshape(q_rank, n_heads * slot).astype(BF16)
    lw["w_uk"] = w_uk[l].reshape(kv_rank, n_heads * nope).astype(BF16)
    lw["w_ukT_h"] = w_uk[l].transpose(1, 2, 0).astype(BF16)
    lw["w_ukT"] = lw["w_ukT_h"].reshape(n_heads * nope, kv_rank)
    lw["w_uv"] = w_uv[l].reshape(kv_rank, -1).astype(BF16)
    lw["w_uv_h"] = w_uv[l].transpose(1, 0, 2).astype(BF16)
    lw["w_ax"] = jnp.concatenate([lru_wa[l], lru_wx[l]], axis=-1).astype(BF16)
    lw["w_out"] = w_out[l].astype(BF16)
    lw["w_up"] = w_up[l].astype(BF16)
    lw["w_down"] = w_down[l].astype(BF16)
    return lw


def _trunk_layer(x_p, x_s, state_conv8, h0_s, page_table, cache_ckv, cache_kpe, lw, tabs, layer, cfg):
    d_rnn, q_rank, kv_rank, n_heads, nope, rope_dim = lw["dims"]
    scale = (nope + rope_dim) ** -0.5
    db = page_table.shape[0]
    t_new = x_s.shape[0] // db

    xr, gate, ckv_p, kpe_p, q, k, v = _prep(x_p, lw, tabs["cos_p"], tabs["sin_p"], prompt=True, tm=cfg["tm"])
    y_rnn, hl_p = _rnn(xr, gate, None, None, lw, chain=True, tt=cfg["tt"])
    y_attn = _flash(q, k, v, n_heads=n_heads, tq=cfg["tq"])
    h, hn = _merge(x_p, y_rnn, y_attn, lw, tm=cfg["tm"])
    y_p = _mlp(h, hn, lw, tm=cfg["tm_mlp"], tf=cfg["tf"])
    conv_p = xr[-(CONV_W - 1):]

    xr, gate, ckv_s, kpe_s, qlat, qpe = _prep(x_s, lw, tabs["cos_s"], tabs["sin_s"], prompt=False,
                                              tm=cfg["tm"])
    y_rnn, hl_s = _rnn(xr, gate, state_conv8, h0_s, lw, chain=False, tt=cfg["tt"])
    olat = _paged(qlat, qpe, ckv_s, kpe_s, page_table, cache_ckv, cache_kpe, lw,
                  (tabs["cos_k"], tabs["sin_k"], tabs["cos_n"], tabs["sin_n"]), layer=layer, scale=scale)
    y_attn = _uvproj(olat, lw["w_uv_h"], t_new=t_new)
    h, hn = _merge(x_s, y_rnn, y_attn, lw, tm=cfg["tm"])
    y_s = _mlp(h, hn, lw, tm=cfg["tm_mlp"], tf=cfg["tf"])
    conv_s = xr.reshape(db, t_new, d_rnn)[:, t_new - (CONV_W - 1):]
    return (y_p, y_s, ckv_p, kpe_p, hl_p[-1], conv_p, ckv_s.reshape(db, t_new, kv_rank),
            kpe_s.reshape(db, t_new, rope_dim), hl_s[:, 0], conv_s)


def kernel(x_prompt, x_sample, cache_ckv, cache_kpe, state_rglru_h, state_conv, page_table, g_mix, w_in,
           conv_w, conv_b, lru_wa, lru_ba, lru_wx, lru_bx, lru_lambda, g_q_a, w_qb, g_kv_a, w_uk, w_uv,
           g_qn, g_kn, g_out_rnn, g_out_attn, w_out, g_mlp, w_up, w_down):
    batch, seq, d_model = x_prompt.shape
    db, t_new, _ = x_sample.shape
    depth = w_in.shape[0]
    assert batch == 1
    n_pages = page_table.shape[1]
    page = cache_ckv.shape[2]
    past = n_pages * page
    rope_dim = cache_kpe.shape[3]
    d_rnn = state_rglru_h.shape[2]
    kb = PAGES_PER_BLOCK * page

    cos_p, sin_p = _token_tables(jnp.arange(seq, dtype=jnp.int32), rope_dim)
    pos_s = past + jnp.arange(t_new, dtype=jnp.int32)
    cos_s, sin_s = _token_tables(jnp.tile(pos_s, db), rope_dim)
    cos_k, sin_k = _key_tables(jnp.arange(past, dtype=jnp.int32), rope_dim, kb)
    cos_n, sin_n = _key_tables(past + jnp.arange(page, dtype=jnp.int32), rope_dim, page)
    tabs = dict(cos_p=cos_p, sin_p=sin_p, cos_s=cos_s, sin_s=sin_s, cos_k=cos_k, sin_k=sin_k,
                cos_n=cos_n[0], sin_n=sin_n[0])
    cfg = dict(tm=min(512, seq, db * t_new), tt=min(32, db * t_new // SUBLANES), tq=min(512, seq),
               tm_mlp=min(512, seq, db * t_new), tf=min(1024, w_up.shape[2]))

    vecs = dict(g_mix=g_mix, conv_b=conv_b, lru_ba=lru_ba.reshape(depth, d_rnn),
                lru_bx=lru_bx.reshape(depth, d_rnn), lru_lambda=lru_lambda, g_q_a=g_q_a, g_kv_a=g_kv_a,
                g_qn=g_qn, g_kn=g_kn, g_out_rnn=g_out_rnn, g_out_attn=g_out_attn, g_mlp=g_mlp)
    y_p = x_prompt.reshape(seq, d_model)
    y_s = x_sample.reshape(db * t_new, d_model)
    state_conv8 = jnp.pad(state_conv, ((0, 0), (0, 0), (SUBLANES - (CONV_W - 1), 0), (0, 0)))
    cache_kpe = cache_kpe.transpose(0, 1, 3, 2)
    outs = []
    for l in range(depth):
        lw = _layer_weights(l, w_in, w_qb, w_uk, w_uv, w_out, w_up, w_down, lru_wa, lru_wx, vecs)
        lw["conv_w"] = conv_w[l]
        res = _trunk_layer(y_p, y_s, state_conv8[l], state_rglru_h[l][:, None, :], page_table, cache_ckv,
                           cache_kpe, lw, tabs, l, cfg)
        y_p, y_s = res[0], res[1]
        outs.append(res[2:])
    ckv_p, kpe_p, h_p, conv_p, ckv_s, kpe_s, h_s, conv_s = (jnp.stack(a) for a in zip(*outs))
    return (y_p.reshape(batch, seq, d_model), y_s.reshape(db, t_new, d_model),
            ckv_p[:, None], kpe_p[:, None], h_p, conv_p[:, None], ckv_s, kpe_s, h_s, conv_s)
```

```python
import functools
import math

import jax
import jax.numpy as jnp
from jax import lax
from jax.experimental import pallas as pl
from jax.experimental.pallas import tpu as pltpu

EPS = 1e-6
ROPE_THETA = 10000.0
LRU_C = 8.0
CONV_W = 4
LOG2_E = math.log2(math.e)

SUBLANES = 8
LANES = 128
MXU_DIM = 256
VMEM_LIMIT_BYTES = 56 * 1024 * 1024

BF16 = jnp.bfloat16
F32 = jnp.float32

_NT_DIMS = (((1,), (1,)), ((), ()))


def _dot(a, b):
    return jnp.dot(a, b, preferred_element_type=F32)


def _dot_nt(a, b):
    return lax.dot_general(a, b, _NT_DIMS, preferred_element_type=F32)


def _rms(x, g):
    return x * lax.rsqrt(jnp.mean(x * x, axis=-1, keepdims=True) + EPS) * g


def _const_spec(shape):
    zeros = (0,) * len(shape)
    return pl.BlockSpec(shape, lambda *_: zeros)


def _layer_spec(stacked_shape, layer):
    return pl.BlockSpec((None,) + tuple(stacked_shape[1:]), lambda *_: (layer, 0, 0))


def _params(semantics):
    return pltpu.CompilerParams(dimension_semantics=semantics, vmem_limit_bytes=VMEM_LIMIT_BYTES)


def _head_slot(v0, v1, g0, g1, cos, sin, qk_dim):
    ssq = jnp.sum(v0 * v0, axis=-1, keepdims=True) + jnp.sum(v1 * v1, axis=-1, keepdims=True)
    rinv = lax.rsqrt(ssq * (1.0 / qk_dim) + EPS)
    n0 = v0 * rinv * g0
    n1 = v1 * rinv * g1
    return n0, _rope_slot(n1, cos, sin)


def _rope_slot(n1, cos, sin):
    partner = pltpu.roll(n1, 32, 1) + pltpu.roll(n1, 96, 1)
    return n1 * cos + partner * sin


def _prep_kernel(x_ref, gmix_ref, win_ref, gqa_ref, wqb_ref, gkva_ref, gq0_ref, gq1_ref, gk0_ref,
                 gk1_ref, cos_ref, sin_ref, wk_ref, wv_ref, *out_refs, dims, prompt):
    d_rnn, q_rank, kv_rank, n_heads, nope, rope_dim = dims
    qk_dim = nope + rope_dim
    qscale = qk_dim ** -0.5 * LOG2_E
    slot = 2 * LANES
    if prompt:
        xr_ref, gate_ref, ckv_ref, kpe_ref, q_ref, k_ref, v_ref = out_refs
    else:
        xr_ref, gate_ref, ckv_ref, kpe_ref, qlat_ref, qpe_ref = out_refs

    xn = _rms(x_ref[...], gmix_ref[...]).astype(BF16)
    o = 0
    xr_ref[...] = _dot(xn, win_ref[:, o:o + d_rnn]); o += d_rnn
    gate_ref[...] = _dot(xn, win_ref[:, o:o + d_rnn]); o += d_rnn
    qa = _dot(xn, win_ref[:, o:o + q_rank]); o += q_rank
    kvc = _dot(xn, win_ref[:, o:o + kv_rank]); o += kv_rank
    kpe = _dot(xn, win_ref[:, o:o + LANES])

    cos = cos_ref[...]
    sin = sin_ref[...]
    ckv = _rms(kvc, gkva_ref[...])
    ckv_ref[...] = ckv
    kpe_ref[...] = kpe[:, :rope_dim]

    qf = _dot(_rms(qa, gqa_ref[...]).astype(BF16), wqb_ref[...])
    gq0, gq1 = gq0_ref[...], gq1_ref[...]
    gk0, gk1 = gk0_ref[...], gk1_ref[...]
    for h in range(n_heads):
        v0 = qf[:, h * slot:h * slot + LANES]
        v1 = qf[:, h * slot + LANES:(h + 1) * slot]
        n0, r1 = _head_slot(v0, v1, gq0, gq1, cos, sin, qk_dim)
        if prompt:
            q_ref[:, h * slot:h * slot + LANES] = (n0 * qscale).astype(BF16)
            q_ref[:, h * slot + LANES:(h + 1) * slot] = (r1 * qscale).astype(BF16)
        else:
            qlat_ref[h] = _dot((n0 * gk0).astype(BF16), wk_ref[h])
            qpe_ref[h] = r1[:, :rope_dim]

    if prompt:
        ckv_bf = ckv.astype(BF16)
        v_ref[...] = _dot(ckv_bf, wv_ref[...]).astype(BF16)
        kn = _dot(ckv_bf, wk_ref[...])
        ssq_pe = jnp.sum(kpe * kpe, axis=-1, keepdims=True)
        kr = _rope_slot(kpe * gk1, cos, sin)
        for h in range(n_heads):
            kh = kn[:, h * nope:(h + 1) * nope]
            ssq = jnp.sum(kh * kh, axis=-1, keepdims=True) + ssq_pe
            rinv = lax.rsqrt(ssq * (1.0 / qk_dim) + EPS)
            k_ref[:, h * slot:h * slot + LANES] = (kh * rinv * gk0).astype(BF16)
            k_ref[:, h * slot + LANES:(h + 1) * slot] = (kr * rinv).astype(BF16)


def _prep(x, lw, cos, sin, *, prompt, tm):
    n, d_model = x.shape
    dims = lw["dims"]
    d_rnn, q_rank, kv_rank, n_heads, nope, rope_dim = dims
    slot = 2 * LANES
    assert nope == LANES and rope_dim == LANES // 2 and n % tm == 0
    row = lambda w: pl.BlockSpec((tm, w), lambda i: (i, 0))
    wk = lw["w_uk"] if prompt else lw["w_ukT_h"]
    in_specs = [row(d_model), _const_spec((1, d_model)), _layer_spec(lw["w_in"].shape, lw["layer"]),
                _const_spec((1, q_rank)), _const_spec(lw["w_qb"].shape), _const_spec((1, kv_rank)),
                _const_spec((1, LANES)), _const_spec((1, LANES)), _const_spec((1, LANES)),
                _const_spec((1, LANES)), row(LANES), row(LANES), _const_spec(wk.shape),
                _const_spec(lw["w_uv"].shape)]
    out_shape = [jax.ShapeDtypeStruct((n, d_rnn), F32), jax.ShapeDtypeStruct((n, d_rnn), F32),
                 jax.ShapeDtypeStruct((n, kv_rank), F32), jax.ShapeDtypeStruct((n, rope_dim), F32)]
    out_specs = [row(d_rnn), row(d_rnn), row(kv_rank), row(rope_dim)]
    if prompt:
        out_shape += [jax.ShapeDtypeStruct((n, n_heads * slot), BF16),
                      jax.ShapeDtypeStruct((n, n_heads * slot), BF16),
                      jax.ShapeDtypeStruct((n, n_heads * nope), BF16)]
        out_specs += [row(n_heads * slot), row(n_heads * slot), row(n_heads * nope)]
    else:
        out_shape += [jax.ShapeDtypeStruct((n_heads, n, kv_rank), F32),
                      jax.ShapeDtypeStruct((n_heads, n, rope_dim), F32)]
        out_specs += [pl.BlockSpec((n_heads, tm, kv_rank), lambda i: (0, i, 0)),
                      pl.BlockSpec((n_heads, tm, rope_dim), lambda i: (0, i, 0))]
    return pl.pallas_call(
        functools.partial(_prep_kernel, dims=dims, prompt=prompt),
        grid=(n // tm,), in_specs=in_specs, out_specs=out_specs, out_shape=out_shape,
        compiler_params=_params(("parallel",)),
        name="prep_prompt" if prompt else "prep_sample",
    )(x, lw["g_mix"], lw["w_in"], lw["g_q_a"], lw["w_qb"], lw["g_kv_a"], lw["gq0"], lw["gq1"],
      lw["gk0"], lw["gk1"], cos, sin, wk, lw["w_uv"])


def _gelu_tanh(x):
    c = math.sqrt(2.0 / math.pi)
    return 0.5 * x * (1.0 + jnp.tanh(c * (x + 0.044715 * (x * x * x))))


def _rnn_kernel(x_ref, gate_ref, prev_ref, h0_ref, cw_ref, cb_ref, wax_ref, ba_ref, bx_ref, lam_ref,
                gout_ref, y_ref, hlast_ref, prev_scr, carry_scr, y_scr, *, chain, n_blocks):
    tt, _, c = x_ref.shape
    blk = c // n_blocks
    x = x_ref[...]
    if chain:
        @pl.when(pl.program_id(0) == 0)
        def _():
            prev_scr[...] = jnp.zeros_like(prev_scr)
            carry_scr[...] = jnp.zeros_like(carry_scr)
        xprev = jnp.concatenate([prev_scr[...], x[:-1]], axis=0)
        prev_scr[...] = x[-1:]
    else:
        xprev = prev_ref[...]

    t_idx = lax.broadcasted_iota(jnp.int32, (tt, SUBLANES, blk), 1)
    softplus_neg_lam = jnp.logaddexp(-lam_ref[...], 0.0)
    ysq = jnp.zeros((tt, SUBLANES, 1), F32)
    for k in range(n_blocks):
        cs = slice(k * blk, (k + 1) * blk)
        xk, xpk = x[:, :, cs], xprev[:, :, cs]
        u = cb_ref[:, cs] + cw_ref[CONV_W - 1:CONV_W, cs] * xk
        for s in range(1, CONV_W):
            sh = jnp.where(t_idx >= s, pltpu.roll(xk, s, 1), pltpu.roll(xpk, s, 1))
            u = u + cw_ref[CONV_W - 1 - s:CONV_W - s, cs] * sh
        u2 = u.reshape(tt * SUBLANES, blk)
        gates = _dot(u2.astype(BF16), wax_ref[k])
        r = jax.nn.sigmoid(gates[:, :blk] + ba_ref[:, cs])
        i = jax.nn.sigmoid(gates[:, blk:] + bx_ref[:, cs])
        log_a = (-LRU_C * r) * softplus_neg_lam[:, cs]
        a = jnp.exp(log_a)
        one_minus_a2 = -jnp.tanh(log_a) * (a * a + 1.0)
        b = (jnp.sqrt(one_minus_a2) * (i * u2)).reshape(tt, SUBLANES, blk)
        a = a.reshape(tt, SUBLANES, blk)
        for s in (1, 2, 4):
            m = t_idx >= s
            a_sh = jnp.where(m, pltpu.roll(a, s, 1), 1.0)
            b_sh = jnp.where(m, pltpu.roll(b, s, 1), 0.0)
            b = a * b_sh + b
            a = a * a_sh
        if chain:
            carry = carry_scr[:, cs]
            carries = []
            for j in range(tt):
                carries.append(carry)
                carry = a[j, SUBLANES - 1:, :] * carry + b[j, SUBLANES - 1:, :]
            carry_scr[:, cs] = carry
            hlast_ref[:, cs] = carry
            h_in = jnp.stack(carries, axis=0)
        else:
            h_in = h0_ref[:, :, cs]
        h = a * h_in + b
        if not chain:
            hlast_ref[:, :, cs] = h[:, SUBLANES - 1:, :]
        y = h * _gelu_tanh(gate_ref[:, :, cs])
        ysq = ysq + jnp.sum(y * y, axis=-1, keepdims=True)
        y_scr[:, :, cs] = y
    rinv = lax.rsqrt(ysq * (1.0 / c) + EPS)
    y_ref[...] = (y_scr[...] * rinv * gout_ref[...]).reshape(tt * SUBLANES, c).astype(BF16)


def _rnn(xr, gate, prev8, h0, lw, *, chain, tt):
    n, c = xr.shape
    nt = n // SUBLANES
    assert nt % tt == 0
    n_blocks = lw["w_ax"].shape[0]
    x3 = xr.reshape(nt, SUBLANES, c)
    g3 = gate.reshape(nt, SUBLANES, c)
    tile = pl.BlockSpec((tt, SUBLANES, c), lambda i: (i, 0, 0))
    if chain:
        prev8 = jnp.zeros((1, SUBLANES, c), F32)
        h0 = jnp.zeros((1, 1, c), F32)
        prev_spec, h0_spec = _const_spec((1, SUBLANES, c)), _const_spec((1, 1, c))
        hl_shape = jax.ShapeDtypeStruct((nt // tt, 1, c), F32)
        hl_spec = pl.BlockSpec((None, 1, c), lambda i: (i, 0, 0))
    else:
        prev_spec = tile
        h0_spec = pl.BlockSpec((tt, 1, c), lambda i: (i, 0, 0))
        hl_shape = jax.ShapeDtypeStruct((nt, 1, c), F32)
        hl_spec = pl.BlockSpec((tt, 1, c), lambda i: (i, 0, 0))
    y, hlast = pl.pallas_call(
        functools.partial(_rnn_kernel, chain=chain, n_blocks=n_blocks),
        grid=(nt // tt,),
        in_specs=[tile, tile, prev_spec, h0_spec, _const_spec((CONV_W, c)), _const_spec((1, c)),
                  _const_spec(lw["w_ax"].shape), _const_spec((1, c)), _const_spec((1, c)),
                  _const_spec((1, c)), _const_spec((1, c))],
        out_specs=[pl.BlockSpec((tt * SUBLANES, c), lambda i: (i, 0)), hl_spec],
        out_shape=[jax.ShapeDtypeStruct((n, c), BF16), hl_shape],
        scratch_shapes=[pltpu.VMEM((1, SUBLANES, c), F32), pltpu.VMEM((1, c), F32),
                        pltpu.VMEM((tt, SUBLANES, c), F32)],
        compiler_params=_params(("arbitrary",)),
        name="rnn_chain" if chain else "rnn_tiles",
    )(x3, g3, prev8, h0, lw["conv_w"], lw["conv_b"], lw["w_ax"], lw["lru_ba"], lw["lru_bx"],
      lw["lru_lambda"], lw["g_out_rnn"])
    return y, hlast


def _lane_partial_sum(p):
    out = p[:, 0:LANES]
    for c in range(1, p.shape[1] // LANES):
        out = out + p[:, c * LANES:(c + 1) * LANES]
    return out


def _softmax_update(s, values, carry):
    m, l, acc = carry
    m_new = jnp.maximum(m, jnp.max(s, axis=-1, keepdims=True))
    p = jnp.exp2(s - m_new)
    corr = jnp.exp2(m - m_new)
    l = l * corr + _lane_partial_sum(p)
    acc = acc * corr + _dot(p.astype(BF16), values)
    return m_new, l, acc


def _softmax_init(rows, width):
    return (jnp.full((rows, 1), -jnp.inf, F32), jnp.zeros((rows, LANES), F32), jnp.zeros((rows, width), F32))


def _softmax_finish(carry):
    _, l, acc = carry
    return acc / jnp.sum(l, axis=-1, keepdims=True)


FLASH_ROW_CHUNK = 64


def _flash_kernel(q_ref, k_ref, v_ref, o_ref, s_scr, p_scr, corr_scr, m_scr, l_scr, acc_scr, *, tk):
    tq = q_ref.shape[0]
    assert tq == 2 * tk
    qi = pl.program_id(1)

    def rows_of(j):
        return pl.ds(pl.multiple_of(j * tk, tk), tk)

    def scores(j, slot, row0=0):
        s_scr[slot, row0:, :] = _dot_nt(q_ref[row0:, :], k_ref[rows_of(j), :])

    def softmax(slot, col_offset=None, row0=0):
        for r in range(row0, tq, FLASH_ROW_CHUNK):
            rs = slice(r, r + FLASH_ROW_CHUNK)
            s = s_scr[slot, rs, :]
            if col_offset is not None and r < col_offset + tk - 1:
                row = r + lax.broadcasted_iota(jnp.int32, s.shape, 0)
                col = col_offset + lax.broadcasted_iota(jnp.int32, s.shape, 1)
                s = jnp.where(col <= row, s, -jnp.inf)
            m = m_scr[rs, :]
            m_new = jnp.maximum(m, jnp.max(s, axis=-1, keepdims=True))
            corr = jnp.exp2(m - m_new)
            m_scr[rs, :] = m_new
            corr_scr[slot, rs, :] = corr
            l = l_scr[rs, :] * corr
            for c in range(0, tk, LANES):
                p = jnp.exp2(s[:, c:c + LANES] - m_new)
                l = l + p
                p_scr[slot, rs, c:c + LANES] = p.astype(BF16)
            l_scr[rs, :] = l

    def values(j, slot, row0=0):
        v = v_ref[rows_of(jnp.maximum(j, 0)), :]
        acc_scr[row0:, :] = acc_scr[row0:, :] * corr_scr[slot, row0:, :] + _dot(p_scr[slot, row0:, :], v)

    m_scr[...] = jnp.full(m_scr.shape, -jnp.inf, F32)
    l_scr[...] = jnp.zeros(l_scr.shape, F32)
    acc_scr[...] = jnp.zeros(acc_scr.shape, F32)
    p_scr[1] = jnp.zeros(p_scr.shape[1:], BF16)
    corr_scr[1] = jnp.ones(corr_scr.shape[1:], F32)
    scores(0, 0)

    def body(t, _):
        j = 2 * t
        scores(j + 1, 1)
        softmax(0)
        values(j - 1, 1)
        scores(j + 2, 0)
        softmax(1)
        values(j, 0)
        return 0

    lax.fori_loop(0, qi, body, 0)
    j = 2 * qi
    scores(j + 1, 1, row0=tk)
    softmax(0, col_offset=0)
    values(j - 1, 1)
    softmax(1, col_offset=tk, row0=tk)
    values(j, 0)
    values(j + 1, 1, row0=tk)
    o_ref[...] = acc_scr[...] / jnp.sum(l_scr[...], axis=-1, keepdims=True)


def _flash(q, k, v, *, n_heads, tq):
    s = q.shape[0]
    slot = q.shape[1] // n_heads
    vd = v.shape[1] // n_heads
    assert s % tq == 0
    tk = tq // 2
    return pl.pallas_call(
        functools.partial(_flash_kernel, tk=tk),
        grid=(n_heads, s // tq),
        in_specs=[pl.BlockSpec((tq, slot), lambda h, i: (i, h)),
                  pl.BlockSpec((s, slot), lambda h, i: (0, h)),
                  pl.BlockSpec((s, vd), lambda h, i: (0, h))],
        out_specs=pl.BlockSpec((tq, vd), lambda h, i: (i, h)),
        out_shape=jax.ShapeDtypeStruct((s, n_heads * vd), F32),
        scratch_shapes=[pltpu.VMEM((2, tq, tk), F32), pltpu.VMEM((2, tq, tk), BF16),
                        pltpu.VMEM((2, tq, LANES), F32), pltpu.VMEM((tq, LANES), F32),
                        pltpu.VMEM((tq, LANES), F32), pltpu.VMEM((tq, vd), F32)],
        compiler_params=_params(("parallel", "arbitrary")),
        name="flash_prompt",
    )(q, k, v)


PAGES_PER_BLOCK = 16
N_SLOTS = 4


def _paged_kernel(pt_ref, wukT_ref, qlat_ref, qpe_ref, cnew_ref, kpenew_ref, g1_ref, g2_ref, cos_ref,
                  sin_ref, cosn_ref, sinn_ref, ckv_hbm, kpe_hbm, o_ref, wcat, ckv_buf, kpe_buf, s_scr, sems,
                  *, layer, scale, qk_dim, page, n_steps):
    n_heads, t_new, rank = qlat_ref.shape
    rope_dim = qpe_ref.shape[2]
    half = rope_dim // 2
    hrows = wukT_ref.shape[0]
    nope = hrows // n_heads
    b = pl.program_id(0)
    total = pl.num_programs(0) * n_steps

    def copies(g, slot):
        out = []
        for p in range(PAGES_PER_BLOCK):
            pid = pt_ref[g * PAGES_PER_BLOCK + p]
            out.append(pltpu.make_async_copy(ckv_hbm.at[layer, pid],
                                             ckv_buf.at[slot, pl.ds(p * page, page), :],
                                             sems.at[slot, 2 * p]))
            out.append(pltpu.make_async_copy(kpe_hbm.at[layer, pid], kpe_buf.at[slot, p],
                                             sems.at[slot, 2 * p + 1]))
        return out

    def issue(g):
        nxt = g + (N_SLOTS - 1)
        for c in copies(jnp.minimum(nxt, total - 1), nxt % N_SLOTS):
            c.start()

    def wait(g):
        for c in copies(g, g % N_SLOTS):
            c.wait()

    g0 = b * n_steps

    @pl.when(b == 0)
    def _():
        wcat[0:hrows, :] = wukT_ref[...]
        for g in range(N_SLOTS - 1):
            for c in copies(g, g):
                c.start()

    wcat[hrows:, :] = qlat_ref[...].reshape(n_heads * t_new, rank).astype(BF16)
    qpe = qpe_ref[...].reshape(n_heads * t_new, rope_dim).astype(BF16)

    def scores(c_bf, kpe_t, g1, g2, cos, sin):
        kt = _dot_nt(wcat[...], c_bf)
        ssq_pe = jnp.sum(kpe_t * kpe_t, axis=0, keepdims=True)
        x1 = kpe_t[:half] * g1
        x2 = kpe_t[half:] * g2
        kr = jnp.concatenate([x1 * cos - x2 * sin, x2 * cos + x1 * sin], axis=0).astype(BF16)
        s_raw = kt[hrows:] + _dot(qpe, kr)
        parts = []
        for h in range(n_heads):
            kh = kt[h * nope:(h + 1) * nope]
            ssq = jnp.sum(kh * kh, axis=0, keepdims=True) + ssq_pe
            rinv = lax.rsqrt(ssq * (1.0 / qk_dim) + EPS) * (scale * LOG2_E)
            parts.append(s_raw[h * t_new:(h + 1) * t_new] * rinv)
        return jnp.concatenate(parts, axis=0)

    def page_scores(g, i):
        slot = g % N_SLOTS
        kpe_t = jnp.concatenate([kpe_buf[slot, p] for p in range(PAGES_PER_BLOCK)], axis=1)
        return scores(ckv_buf[slot].astype(BF16), kpe_t, g1_ref[...], g2_ref[...], cos_ref[i], sin_ref[i])

    def page_update(g, s, carry):
        return _softmax_update(s, ckv_buf[g % N_SLOTS].astype(BF16), carry)

    wait(g0)
    s_scr[...] = page_scores(g0, 0)

    rows = n_heads * t_new
    c_new = jnp.concatenate([cnew_ref[...], jnp.zeros((page - t_new, rank), F32)], axis=0).astype(BF16)
    s = scores(c_new, kpenew_ref[...], g1_ref[:, 0:page], g2_ref[:, 0:page], cosn_ref[...], sinn_ref[...])
    key = lax.broadcasted_iota(jnp.int32, s.shape, 1)
    qt = lax.broadcasted_iota(jnp.int32, s.shape, 0) % t_new
    carry = _softmax_update(jnp.where(key <= qt, s, -jnp.inf), c_new, _softmax_init(rows, rank))

    def step(i, carry):
        g = g0 + i
        wait(g + 1)
        s = s_scr[...]
        s_scr[...] = page_scores(g + 1, i + 1)
        issue(g)
        return page_update(g, s, carry)

    carry = lax.fori_loop(0, n_steps - 1, step, carry)
    g_last = g0 + n_steps - 1
    issue(g_last)
    o_ref[...] = _softmax_finish(page_update(g_last, s_scr[...], carry))

    @pl.when(b == pl.num_programs(0) - 1)
    def _():
        for k in range(1, N_SLOTS):
            for c in copies(total - 1, (total - 1 + k) % N_SLOTS):
                c.wait()


def _paged(qlat, qpe, ckv_new, kpe_new, page_table, cache_ckv, cache_kpe_t, lw, tabs, *, layer, scale):
    n_heads, n, rank = qlat.shape
    rope_dim = qpe.shape[2]
    db, n_pages = page_table.shape
    t_new = n // db
    page = cache_ckv.shape[2]
    assert t_new == SUBLANES and page == LANES and n_pages % PAGES_PER_BLOCK == 0
    n_steps = n_pages // PAGES_PER_BLOCK
    assert db * n_steps >= N_SLOTS
    kb = PAGES_PER_BLOCK * page
    half = rope_dim // 2
    hrows = lw["w_ukT"].shape[0]
    rows = n_heads * t_new
    cos_t, sin_t, cos_n, sin_n = tabs
    kpe_new_t = jnp.pad(kpe_new.reshape(db, t_new, rope_dim).transpose(0, 2, 1),
                        ((0, 0), (0, 0), (0, page - t_new)))
    grid_spec = pltpu.PrefetchScalarGridSpec(
        num_scalar_prefetch=1, grid=(db,),
        in_specs=[pl.BlockSpec((hrows, rank), lambda b, pt: (0, 0)),
                  pl.BlockSpec((n_heads, t_new, rank), lambda b, pt: (0, b, 0)),
                  pl.BlockSpec((n_heads, t_new, rope_dim), lambda b, pt: (0, b, 0)),
                  pl.BlockSpec((t_new, rank), lambda b, pt: (b, 0)),
                  pl.BlockSpec((None, rope_dim, page), lambda b, pt: (b, 0, 0)),
                  pl.BlockSpec((half, kb), lambda b, pt: (0, 0)),
                  pl.BlockSpec((half, kb), lambda b, pt: (0, 0)),
                  pl.BlockSpec((n_steps, half, kb), lambda b, pt: (0, 0, 0)),
                  pl.BlockSpec((n_steps, half, kb), lambda b, pt: (0, 0, 0)),
                  pl.BlockSpec((half, page), lambda b, pt: (0, 0)),
                  pl.BlockSpec((half, page), lambda b, pt: (0, 0)),
                  pl.BlockSpec(memory_space=pl.ANY),
                  pl.BlockSpec(memory_space=pl.ANY)],
        out_specs=pl.BlockSpec((None, rows, rank), lambda b, pt: (b, 0, 0)),
        scratch_shapes=[pltpu.VMEM((hrows + rows, rank), BF16),
                        pltpu.VMEM((N_SLOTS, kb, rank), F32),
                        pltpu.VMEM((N_SLOTS, PAGES_PER_BLOCK, rope_dim, page), F32),
                        pltpu.VMEM((rows, kb), F32),
                        pltpu.SemaphoreType.DMA((N_SLOTS, 2 * PAGES_PER_BLOCK))])
    return pl.pallas_call(
        functools.partial(_paged_kernel, layer=layer, scale=scale, qk_dim=lw["dims"][4] + rope_dim,
                          page=page, n_steps=n_steps),
        grid_spec=grid_spec,
        out_shape=jax.ShapeDtypeStruct((db, rows, rank), F32),
        compiler_params=_params(("arbitrary",)),
        name="paged_sample",
    )(page_table.reshape(-1), lw["w_ukT"], qlat, qpe, ckv_new, kpe_new_t, lw["gk_pe1"], lw["gk_pe2"], cos_t, sin_t,
      cos_n, sin_n, cache_ckv, cache_kpe_t)


def _uvproj_kernel(olat_ref, wuv_ref, o_ref):
    db, t_new, rank = olat_ref.shape
    o_ref[...] = _dot(olat_ref[...].reshape(db * t_new, rank).astype(BF16), wuv_ref[...])


def _uvproj(olat, w_uv_h, *, t_new):
    db, rows, rank = olat.shape
    n_heads, _, vd = w_uv_h.shape
    olat4 = olat.reshape(db, n_heads, t_new, rank)
    return pl.pallas_call(
        _uvproj_kernel,
        grid=(n_heads,),
        in_specs=[pl.BlockSpec((db, None, t_new, rank), lambda h: (0, h, 0, 0)),
                  pl.BlockSpec((None, rank, vd), lambda h: (h, 0, 0))],
        out_specs=pl.BlockSpec((db * t_new, vd), lambda h: (0, h)),
        out_shape=jax.ShapeDtypeStruct((db * t_new, n_heads * vd), F32),
        compiler_params=_params(("parallel",)),
        name="uvproj_sample",
    )(olat4, w_uv_h)


def _merge_kernel(x_ref, yr_ref, ya_ref, gattn_ref, wout_ref, gmlp_ref, h_ref, hn_ref):
    d_rnn = yr_ref.shape[1]
    yan = _rms(ya_ref[...], gattn_ref[...]).astype(BF16)
    h = x_ref[...] + _dot(yr_ref[...], wout_ref[0:d_rnn, :]) + _dot(yan, wout_ref[d_rnn:, :])
    h_ref[...] = h
    hn_ref[...] = _rms(h, gmlp_ref[...]).astype(BF16)


def _merge(x, y_rnn, y_attn, lw, *, tm):
    n, d_model = x.shape
    d_rnn, d_attn = y_rnn.shape[1], y_attn.shape[1]
    row = lambda w: pl.BlockSpec((tm, w), lambda i: (i, 0))
    return pl.pallas_call(
        _merge_kernel,
        grid=(n // tm,),
        in_specs=[row(d_model), row(d_rnn), row(d_attn), _const_spec((1, d_attn)),
                  _layer_spec(lw["w_out"].shape, lw["layer"]), _const_spec((1, d_model))],
        out_specs=[row(d_model), row(d_model)],
        out_shape=[jax.ShapeDtypeStruct((n, d_model), F32), jax.ShapeDtypeStruct((n, d_model), BF16)],
        compiler_params=_params(("parallel",)),
        name="merge",
    )(x, y_rnn, y_attn, lw["g_out_attn"], lw["w_out"], lw["g_mlp"])


def _mlp_kernel(h_ref, hn_ref, wup_ref, wdn_ref, o_ref):
    @pl.when(pl.program_id(1) == 0)
    def _():
        o_ref[...] = h_ref[...]
    up = jnp.maximum(_dot(hn_ref[...], wup_ref[...]), 0.0)
    o_ref[...] += _dot((up * up).astype(BF16), wdn_ref[...])


def _mlp(h, hn, lw, *, tm, tf):
    n, d_model = h.shape
    d_ff = lw["w_up"].shape[2]
    layer = lw["layer"]
    assert n % tm == 0 and d_ff % tf == 0
    return pl.pallas_call(
        _mlp_kernel,
        grid=(n // tm, d_ff // tf),
        in_specs=[pl.BlockSpec((tm, d_model), lambda i, f: (i, 0)),
                  pl.BlockSpec((tm, d_model), lambda i, f: (i, 0)),
                  pl.BlockSpec((None, d_model, tf), lambda i, f: (layer, 0, f)),
                  pl.BlockSpec((None, tf, d_model), lambda i, f: (layer, f, 0))],
        out_specs=pl.BlockSpec((tm, d_model), lambda i, f: (i, 0)),
        out_shape=jax.ShapeDtypeStruct((n, d_model), F32),
        compiler_params=_params(("parallel", "arbitrary")),
        name="mlp",
    )(h, hn, lw["w_up"], lw["w_down"])


def _rope_angles(pos, rope_dim):
    half = rope_dim // 2
    inv_freq = ROPE_THETA ** (-jnp.arange(half, dtype=F32) / half)
    return pos.astype(F32)[:, None] * inv_freq[None, :]


def _token_tables(pos, rope_dim):
    ang = _rope_angles(pos, rope_dim)
    cos, sin = jnp.cos(ang), jnp.sin(ang)
    pad = jnp.zeros((pos.shape[0], LANES - rope_dim), F32)
    return jnp.concatenate([cos, cos, pad], axis=1), jnp.concatenate([-sin, sin, pad], axis=1)


def _key_tables(pos, rope_dim, block):
    ang = _rope_angles(pos, rope_dim).T
    half = rope_dim // 2
    nb = pos.shape[0] // block
    to_blocks = lambda a: a.reshape(half, nb, block).transpose(1, 0, 2)
    return to_blocks(jnp.cos(ang)), to_blocks(jnp.sin(ang))


def _stacked_weights(w_in, w_out, w_up, w_down, rope_dim):
    depth, d_model, _ = w_in.shape
    w_in = jnp.concatenate([w_in, jnp.zeros((depth, d_model, LANES - rope_dim), F32)], axis=2)
    return dict(w_in=w_in.astype(BF16), w_out=w_out.astype(BF16), w_up=w_up.astype(BF16),
                w_down=w_down.astype(BF16))


def _layer_weights(l, stacked, w_qb, w_uk, w_uv, lru_wa, lru_wx, vecs):
    kv_rank, n_heads, nope = w_uk.shape[1:]
    q_rank = w_qb.shape[1]
    d_rnn = lru_wa.shape[1] * lru_wa.shape[2]
    qk_dim = w_qb.shape[2] // n_heads
    rope_dim = qk_dim - nope
    half = rope_dim // 2
    slot = 2 * LANES
    kb = PAGES_PER_BLOCK * LANES
    v = {k: a[l][None, :] for k, a in vecs.items()}
    g_qn, g_kn = v.pop("g_qn"), v.pop("g_kn")
    pad = jnp.zeros((1, LANES - rope_dim), F32)
    lw = dict(v)
    lw.update(stacked)
    lw["layer"] = l
    lw["dims"] = (d_rnn, q_rank, kv_rank, n_heads, nope, rope_dim)
    lw["gq0"], lw["gq1"] = g_qn[:, :nope], jnp.concatenate([g_qn[:, nope:], pad], axis=1)
    lw["gk0"], lw["gk1"] = g_kn[:, :nope], jnp.concatenate([g_kn[:, nope:], pad], axis=1)
    lw["gk_pe1"] = jnp.broadcast_to(g_kn[0, nope:nope + half][:, None], (half, kb))
    lw["gk_pe2"] = jnp.broadcast_to(g_kn[0, nope + half:][:, None], (half, kb))
    wq = w_qb[l].reshape(q_rank, n_heads, qk_dim)
    wq = jnp.concatenate([wq, jnp.zeros((q_rank, n_heads, slot - qk_dim), F32)], axis=2)
    lw["w_qb"] = wq.reshape(q_rank, n_heads * slot).astype(BF16)
    lw["w_uk"] = w_uk[l].reshape(kv_rank, n_heads * nope).astype(BF16)
    lw["w_ukT_h"] = w_uk[l].transpose(1, 2, 0).astype(BF16)
    lw["w_ukT"] = lw["w_ukT_h"].reshape(n_heads * nope, kv_rank)
    lw["w_uv"] = w_uv[l].reshape(kv_rank, -1).astype(BF16)
    lw["w_uv_h"] = w_uv[l].transpose(1, 0, 2).astype(BF16)
    lw["w_ax"] = jnp.concatenate([lru_wa[l], lru_wx[l]], axis=-1).astype(BF16)
    return lw


def _trunk_layer(x_p, x_s, state_conv8, h0_s, page_table, cache_ckv, cache_kpe, lw, tabs, layer, cfg):
    d_rnn, q_rank, kv_rank, n_heads, nope, rope_dim = lw["dims"]
    scale = (nope + rope_dim) ** -0.5
    db = page_table.shape[0]
    t_new = x_s.shape[0] // db

    xr, gate, ckv_p, kpe_p, q, k, v = _prep(x_p, lw, tabs["cos_p"], tabs["sin_p"], prompt=True, tm=cfg["tm"])
    y_rnn, hl_p = _rnn(xr, gate, None, None, lw, chain=True, tt=cfg["tt"])
    y_attn = _flash(q, k, v, n_heads=n_heads, tq=cfg["tq"])
    h, hn = _merge(x_p, y_rnn, y_attn, lw, tm=cfg["tm"])
    y_p = _mlp(h, hn, lw, tm=cfg["tm_mlp"], tf=cfg["tf"])
    conv_p = xr[-(CONV_W - 1):]

    xr, gate, ckv_s, kpe_s, qlat, qpe = _prep(x_s, lw, tabs["cos_s"], tabs["sin_s"], prompt=False,
                                              tm=cfg["tm"])
    y_rnn, hl_s = _rnn(xr, gate, state_conv8, h0_s, lw, chain=False, tt=cfg["tt"])
    olat = _paged(qlat, qpe, ckv_s, kpe_s, page_table, cache_ckv, cache_kpe, lw,
                  (tabs["cos_k"], tabs["sin_k"], tabs["cos_n"], tabs["sin_n"]), layer=layer, scale=scale)
    y_attn = _uvproj(olat, lw["w_uv_h"], t_new=t_new)
    h, hn = _merge(x_s, y_rnn, y_attn, lw, tm=cfg["tm"])
    y_s = _mlp(h, hn, lw, tm=cfg["tm_mlp"], tf=cfg["tf"])
    conv_s = xr.reshape(db, t_new, d_rnn)[:, t_new - (CONV_W - 1):]
    return (y_p, y_s, ckv_p, kpe_p, hl_p[-1], conv_p, ckv_s.reshape(db, t_new, kv_rank),
            kpe_s.reshape(db, t_new, rope_dim), hl_s[:, 0], conv_s)


def kernel(x_prompt, x_sample, cache_ckv, cache_kpe, state_rglru_h, state_conv, page_table, g_mix, w_in,
           conv_w, conv_b, lru_wa, lru_ba, lru_wx, lru_bx, lru_lambda, g_q_a, w_qb, g_kv_a, w_uk, w_uv,
           g_qn, g_kn, g_out_rnn, g_out_attn, w_out, g_mlp, w_up, w_down):
    batch, seq, d_model = x_prompt.shape
    db, t_new, _ = x_sample.shape
    depth = w_in.shape[0]
    assert batch == 1
    n_pages = page_table.shape[1]
    page = cache_ckv.shape[2]
    past = n_pages * page
    rope_dim = cache_kpe.shape[3]
    d_rnn = state_rglru_h.shape[2]
    kb = PAGES_PER_BLOCK * page

    cos_p, sin_p = _token_tables(jnp.arange(seq, dtype=jnp.int32), rope_dim)
    pos_s = past + jnp.arange(t_new, dtype=jnp.int32)
    cos_s, sin_s = _token_tables(jnp.tile(pos_s, db), rope_dim)
    cos_k, sin_k = _key_tables(jnp.arange(past, dtype=jnp.int32), rope_dim, kb)
    cos_n, sin_n = _key_tables(past + jnp.arange(page, dtype=jnp.int32), rope_dim, page)
    tabs = dict(cos_p=cos_p, sin_p=sin_p, cos_s=cos_s, sin_s=sin_s, cos_k=cos_k, sin_k=sin_k,
                cos_n=cos_n[0], sin_n=sin_n[0])
    cfg = dict(tm=min(512, seq, db * t_new), tt=min(32, db * t_new // SUBLANES), tq=min(512, seq),
               tm_mlp=min(512, seq, db * t_new), tf=min(1024, w_up.shape[2]))

    vecs = dict(g_mix=g_mix, conv_b=conv_b, lru_ba=lru_ba.reshape(depth, d_rnn),
                lru_bx=lru_bx.reshape(depth, d_rnn), lru_lambda=lru_lambda, g_q_a=g_q_a, g_kv_a=g_kv_a,
                g_qn=g_qn, g_kn=g_kn, g_out_rnn=g_out_rnn, g_out_attn=g_out_attn, g_mlp=g_mlp)
    y_p = x_prompt.reshape(seq, d_model)
    y_s = x_sample.reshape(db * t_new, d_model)
    state_conv8 = jnp.pad(state_conv, ((0, 0), (0, 0), (SUBLANES - (CONV_W - 1), 0), (0, 0)))
    cache_kpe = cache_kpe.transpose(0, 1, 3, 2)
    stacked = _stacked_weights(w_in, w_out, w_up, w_down, rope_dim)
    outs = []
    for l in range(depth):
        lw = _layer_weights(l, stacked, w_qb, w_uk, w_uv, lru_wa, lru_wx, vecs)
        lw["conv_w"] = conv_w[l]
        res = _trunk_layer(y_p, y_s, state_conv8[l], state_rglru_h[l][:, None, :], page_table, cache_ckv,
                           cache_kpe, lw, tabs, l, cfg)
        y_p, y_s = res[0], res[1]
        outs.append(res[2:])
    ckv_p, kpe_p, h_p, conv_p, ckv_s, kpe_s, h_s, conv_s = (jnp.stack(a) for a in zip(*outs))
    return (y_p.reshape(batch, seq, d_model), y_s.reshape(db, t_new, d_model),
            ckv_p[:, None], kpe_p[:, None], h_p, conv_p[:, None], ckv_s, kpe_s, h_s, conv_s)
```

```python
import functools
import math

import jax
import jax.numpy as jnp
from jax import lax
from jax.experimental import pallas as pl
from jax.experimental.pallas import tpu as pltpu

EPS = 1e-6
ROPE_THETA = 10000.0
LRU_C = 8.0
CONV_W = 4
LOG2_E = math.log2(math.e)

SUBLANES = 8
LANES = 128
MXU_DIM = 256
VMEM_LIMIT_BYTES = 56 * 1024 * 1024

BF16 = jnp.bfloat16
F32 = jnp.float32

_NT_DIMS = (((1,), (1,)), ((), ()))


def _dot(a, b):
    return jnp.dot(a, b, preferred_element_type=F32)


def _dot_nt(a, b):
    return lax.dot_general(a, b, _NT_DIMS, preferred_element_type=F32)


def _rms(x, g):
    return x * lax.rsqrt(jnp.mean(x * x, axis=-1, keepdims=True) + EPS) * g


def _const_spec(shape):
    zeros = (0,) * len(shape)
    return pl.BlockSpec(shape, lambda *_: zeros)


def _layer_spec(stacked_shape, layer):
    return pl.BlockSpec((None,) + tuple(stacked_shape[1:]), lambda *_: (layer, 0, 0))


def _params(semantics):
    return pltpu.CompilerParams(dimension_semantics=semantics, vmem_limit_bytes=VMEM_LIMIT_BYTES)


def _head_slot(v0, v1, g0, g1, cos, sin, qk_dim):
    ssq = jnp.sum(v0 * v0, axis=-1, keepdims=True) + jnp.sum(v1 * v1, axis=-1, keepdims=True)
    rinv = lax.rsqrt(ssq * (1.0 / qk_dim) + EPS)
    n0 = v0 * rinv * g0
    n1 = v1 * rinv * g1
    return n0, _rope_slot(n1, cos, sin)


def _rope_slot(n1, cos, sin):
    partner = pltpu.roll(n1, 32, 1) + pltpu.roll(n1, 96, 1)
    return n1 * cos + partner * sin


def _prep_kernel(x_ref, gmix_ref, win_ref, gqa_ref, wqb_ref, gkva_ref, gq0_ref, gq1_ref, gk0_ref,
                 gk1_ref, cos_ref, sin_ref, wk_ref, wv_ref, *out_refs, dims, prompt):
    d_rnn, q_rank, kv_rank, n_heads, nope, rope_dim = dims
    qk_dim = nope + rope_dim
    qscale = qk_dim ** -0.5 * LOG2_E
    slot = 2 * LANES
    if prompt:
        xr_ref, gate_ref, ckv_ref, kpe_ref, q_ref, k_ref, v_ref = out_refs
    else:
        xr_ref, gate_ref, ckv_ref, kpe_ref, qlat_ref, qpe_ref = out_refs

    xn = _rms(x_ref[...], gmix_ref[...]).astype(BF16)
    o = 0
    xr_ref[...] = _dot(xn, win_ref[:, o:o + d_rnn]); o += d_rnn
    gate_ref[...] = _dot(xn, win_ref[:, o:o + d_rnn]); o += d_rnn
    qa = _dot(xn, win_ref[:, o:o + q_rank]); o += q_rank
    kvc = _dot(xn, win_ref[:, o:o + kv_rank]); o += kv_rank
    kpe = _dot(xn, win_ref[:, o:o + LANES])

    cos = cos_ref[...]
    sin = sin_ref[...]
    ckv = _rms(kvc, gkva_ref[...])
    ckv_ref[...] = ckv
    kpe_ref[...] = kpe[:, :rope_dim]

    qf = _dot(_rms(qa, gqa_ref[...]).astype(BF16), wqb_ref[...])
    gq0, gq1 = gq0_ref[...], gq1_ref[...]
    gk0, gk1 = gk0_ref[...], gk1_ref[...]
    for h in range(n_heads):
        v0 = qf[:, h * slot:h * slot + LANES]
        v1 = qf[:, h * slot + LANES:(h + 1) * slot]
        n0, r1 = _head_slot(v0, v1, gq0, gq1, cos, sin, qk_dim)
        if prompt:
            q_ref[:, h * slot:h * slot + LANES] = (n0 * qscale).astype(BF16)
            q_ref[:, h * slot + LANES:(h + 1) * slot] = (r1 * qscale).astype(BF16)
        else:
            qlat_ref[h] = _dot((n0 * gk0).astype(BF16), wk_ref[h])
            qpe_ref[h] = r1[:, :rope_dim]

    if prompt:
        ckv_bf = ckv.astype(BF16)
        v_ref[...] = _dot(ckv_bf, wv_ref[...]).astype(BF16)
        kn = _dot(ckv_bf, wk_ref[...])
        ssq_pe = jnp.sum(kpe * kpe, axis=-1, keepdims=True)
        kr = _rope_slot(kpe * gk1, cos, sin)
        for h in range(n_heads):
            kh = kn[:, h * nope:(h + 1) * nope]
            ssq = jnp.sum(kh * kh, axis=-1, keepdims=True) + ssq_pe
            rinv = lax.rsqrt(ssq * (1.0 / qk_dim) + EPS)
            k_ref[:, h * slot:h * slot + LANES] = (kh * rinv * gk0).astype(BF16)
            k_ref[:, h * slot + LANES:(h + 1) * slot] = (kr * rinv).astype(BF16)


def _prep(x, lw, cos, sin, *, prompt, tm):
    n, d_model = x.shape
    dims = lw["dims"]
    d_rnn, q_rank, kv_rank, n_heads, nope, rope_dim = dims
    slot = 2 * LANES
    assert nope == LANES and rope_dim == LANES // 2 and n % tm == 0
    row = lambda w: pl.BlockSpec((tm, w), lambda i: (i, 0))
    wk = lw["w_uk"] if prompt else lw["w_ukT_h"]
    in_specs = [row(d_model), _const_spec((1, d_model)), _layer_spec(lw["w_in"].shape, lw["layer"]),
                _const_spec((1, q_rank)), _const_spec(lw["w_qb"].shape), _const_spec((1, kv_rank)),
                _const_spec((1, LANES)), _const_spec((1, LANES)), _const_spec((1, LANES)),
                _const_spec((1, LANES)), row(LANES), row(LANES), _const_spec(wk.shape),
                _const_spec(lw["w_uv"].shape)]
    out_shape = [jax.ShapeDtypeStruct((n, d_rnn), F32), jax.ShapeDtypeStruct((n, d_rnn), F32),
                 jax.ShapeDtypeStruct((n, kv_rank), F32), jax.ShapeDtypeStruct((n, rope_dim), F32)]
    out_specs = [row(d_rnn), row(d_rnn), row(kv_rank), row(rope_dim)]
    if prompt:
        out_shape += [jax.ShapeDtypeStruct((n, n_heads * slot), BF16),
                      jax.ShapeDtypeStruct((n, n_heads * slot), BF16),
                      jax.ShapeDtypeStruct((n, n_heads * nope), BF16)]
        out_specs += [row(n_heads * slot), row(n_heads * slot), row(n_heads * nope)]
    else:
        out_shape += [jax.ShapeDtypeStruct((n_heads, n, kv_rank), F32),
                      jax.ShapeDtypeStruct((n_heads, n, rope_dim), F32)]
        out_specs += [pl.BlockSpec((n_heads, tm, kv_rank), lambda i: (0, i, 0)),
                      pl.BlockSpec((n_heads, tm, rope_dim), lambda i: (0, i, 0))]
    return pl.pallas_call(
        functools.partial(_prep_kernel, dims=dims, prompt=prompt),
        grid=(n // tm,), in_specs=in_specs, out_specs=out_specs, out_shape=out_shape,
        compiler_params=_params(("parallel",)),
        name="prep_prompt" if prompt else "prep_sample",
    )(x, lw["g_mix"], lw["w_in"], lw["g_q_a"], lw["w_qb"], lw["g_kv_a"], lw["gq0"], lw["gq1"],
      lw["gk0"], lw["gk1"], cos, sin, wk, lw["w_uv"])


def _gelu_tanh(x):
    c = math.sqrt(2.0 / math.pi)
    return 0.5 * x * (1.0 + jnp.tanh(c * (x + 0.044715 * (x * x * x))))


def _rnn_kernel(x_ref, gate_ref, prev_ref, h0_ref, cw_ref, cb_ref, wax_ref, ba_ref, bx_ref, lam_ref,
                gout_ref, y_ref, hlast_ref, prev_scr, carry_scr, y_scr, *, chain, n_blocks):
    tt, _, c = x_ref.shape
    blk = c // n_blocks
    x = x_ref[...]
    if chain:
        @pl.when(pl.program_id(0) == 0)
        def _():
            prev_scr[...] = jnp.zeros_like(prev_scr)
            carry_scr[...] = jnp.zeros_like(carry_scr)
        xprev = jnp.concatenate([prev_scr[...], x[:-1]], axis=0)
        prev_scr[...] = x[-1:]
    else:
        xprev = prev_ref[...]

    t_idx = lax.broadcasted_iota(jnp.int32, (tt, SUBLANES, blk), 1)
    softplus_neg_lam = jnp.logaddexp(-lam_ref[...], 0.0)
    ysq = jnp.zeros((tt, SUBLANES, 1), F32)
    for k in range(n_blocks):
        cs = slice(k * blk, (k + 1) * blk)
        xk, xpk = x[:, :, cs], xprev[:, :, cs]
        u = cb_ref[:, cs] + cw_ref[CONV_W - 1:CONV_W, cs] * xk
        for s in range(1, CONV_W):
            sh = jnp.where(t_idx >= s, pltpu.roll(xk, s, 1), pltpu.roll(xpk, s, 1))
            u = u + cw_ref[CONV_W - 1 - s:CONV_W - s, cs] * sh
        u2 = u.reshape(tt * SUBLANES, blk)
        gates = _dot(u2.astype(BF16), wax_ref[k])
        r = jax.nn.sigmoid(gates[:, :blk] + ba_ref[:, cs])
        i = jax.nn.sigmoid(gates[:, blk:] + bx_ref[:, cs])
        log_a = (-LRU_C * r) * softplus_neg_lam[:, cs]
        a = jnp.exp(log_a)
        one_minus_a2 = -jnp.tanh(log_a) * (a * a + 1.0)
        b = (jnp.sqrt(one_minus_a2) * (i * u2)).reshape(tt, SUBLANES, blk)
        a = a.reshape(tt, SUBLANES, blk)
        for s in (1, 2, 4):
            m = t_idx >= s
            a_sh = jnp.where(m, pltpu.roll(a, s, 1), 1.0)
            b_sh = jnp.where(m, pltpu.roll(b, s, 1), 0.0)
            b = a * b_sh + b
            a = a * a_sh
        if chain:
            carry = carry_scr[:, cs]
            carries = []
            for j in range(tt):
                carries.append(carry)
                carry = a[j, SUBLANES - 1:, :] * carry + b[j, SUBLANES - 1:, :]
            carry_scr[:, cs] = carry
            hlast_ref[:, cs] = carry
            h_in = jnp.stack(carries, axis=0)
        else:
            h_in = h0_ref[:, :, cs]
        h = a * h_in + b
        if not chain:
            hlast_ref[:, :, cs] = h[:, SUBLANES - 1:, :]
        y = h * _gelu_tanh(gate_ref[:, :, cs])
        ysq = ysq + jnp.sum(y * y, axis=-1, keepdims=True)
        y_scr[:, :, cs] = y
    rinv = lax.rsqrt(ysq * (1.0 / c) + EPS)
    y_ref[...] = (y_scr[...] * rinv * gout_ref[...]).reshape(tt * SUBLANES, c).astype(BF16)


def _rnn(xr, gate, prev8, h0, lw, *, chain, tt):
    n, c = xr.shape
    nt = n // SUBLANES
    assert nt % tt == 0
    n_blocks = lw["w_ax"].shape[0]
    x3 = xr.reshape(nt, SUBLANES, c)
    g3 = gate.reshape(nt, SUBLANES, c)
    tile = pl.BlockSpec((tt, SUBLANES, c), lambda i: (i, 0, 0))
    if chain:
        prev8 = jnp.zeros((1, SUBLANES, c), F32)
        h0 = jnp.zeros((1, 1, c), F32)
        prev_spec, h0_spec = _const_spec((1, SUBLANES, c)), _const_spec((1, 1, c))
        hl_shape = jax.ShapeDtypeStruct((nt // tt, 1, c), F32)
        hl_spec = pl.BlockSpec((None, 1, c), lambda i: (i, 0, 0))
    else:
        prev_spec = tile
        h0_spec = pl.BlockSpec((tt, 1, c), lambda i: (i, 0, 0))
        hl_shape = jax.ShapeDtypeStruct((nt, 1, c), F32)
        hl_spec = pl.BlockSpec((tt, 1, c), lambda i: (i, 0, 0))
    y, hlast = pl.pallas_call(
        functools.partial(_rnn_kernel, chain=chain, n_blocks=n_blocks),
        grid=(nt // tt,),
        in_specs=[tile, tile, prev_spec, h0_spec, _const_spec((CONV_W, c)), _const_spec((1, c)),
                  _const_spec(lw["w_ax"].shape), _const_spec((1, c)), _const_spec((1, c)),
                  _const_spec((1, c)), _const_spec((1, c))],
        out_specs=[pl.BlockSpec((tt * SUBLANES, c), lambda i: (i, 0)), hl_spec],
        out_shape=[jax.ShapeDtypeStruct((n, c), BF16), hl_shape],
        scratch_shapes=[pltpu.VMEM((1, SUBLANES, c), F32), pltpu.VMEM((1, c), F32),
                        pltpu.VMEM((tt, SUBLANES, c), F32)],
        compiler_params=_params(("arbitrary",)),
        name="rnn_chain" if chain else "rnn_tiles",
    )(x3, g3, prev8, h0, lw["conv_w"], lw["conv_b"], lw["w_ax"], lw["lru_ba"], lw["lru_bx"],
      lw["lru_lambda"], lw["g_out_rnn"])
    return y, hlast


def _lane_partial_sum(p):
    out = p[:, 0:LANES]
    for c in range(1, p.shape[1] // LANES):
        out = out + p[:, c * LANES:(c + 1) * LANES]
    return out


def _softmax_update(s, values, carry):
    m, l, acc = carry
    m_new = jnp.maximum(m, jnp.max(s, axis=-1, keepdims=True))
    p = jnp.exp2(s - m_new)
    corr = jnp.exp2(m - m_new)
    l = l * corr + _lane_partial_sum(p)
    acc = acc * corr + _dot(p.astype(BF16), values)
    return m_new, l, acc


def _softmax_init(rows, width):
    return (jnp.full((rows, 1), -jnp.inf, F32), jnp.zeros((rows, LANES), F32), jnp.zeros((rows, width), F32))


def _softmax_finish(carry):
    _, l, acc = carry
    return acc / jnp.sum(l, axis=-1, keepdims=True)


FLASH_ROW_CHUNK = 64


def _flash_kernel(q_ref, k_ref, v_ref, o_ref, s_scr, p_scr, corr_scr, m_scr, l_scr, acc_scr, *, tk):
    tq = q_ref.shape[0]
    assert tq == 2 * tk
    qi = pl.program_id(1)

    def rows_of(j):
        return pl.ds(pl.multiple_of(j * tk, tk), tk)

    def scores(j, slot, row0=0):
        s_scr[slot, row0:, :] = _dot_nt(q_ref[row0:, :], k_ref[rows_of(j), :])

    def softmax(slot, col_offset=None, row0=0):
        for r in range(row0, tq, FLASH_ROW_CHUNK):
            rs = slice(r, r + FLASH_ROW_CHUNK)
            s = s_scr[slot, rs, :]
            if col_offset is not None and r < col_offset + tk - 1:
                row = r + lax.broadcasted_iota(jnp.int32, s.shape, 0)
                col = col_offset + lax.broadcasted_iota(jnp.int32, s.shape, 1)
                s = jnp.where(col <= row, s, -jnp.inf)
            m = m_scr[rs, :]
            m_new = jnp.maximum(m, jnp.max(s, axis=-1, keepdims=True))
            corr = jnp.exp2(m - m_new)
            m_scr[rs, :] = m_new
            corr_scr[slot, rs, :] = corr
            l = l_scr[rs, :] * corr
            for c in range(0, tk, LANES):
                p = jnp.exp2(s[:, c:c + LANES] - m_new)
                l = l + p
                p_scr[slot, rs, c:c + LANES] = p.astype(BF16)
            l_scr[rs, :] = l

    def values(j, slot, row0=0):
        v = v_ref[rows_of(jnp.maximum(j, 0)), :]
        acc_scr[row0:, :] = acc_scr[row0:, :] * corr_scr[slot, row0:, :] + _dot(p_scr[slot, row0:, :], v)

    m_scr[...] = jnp.full(m_scr.shape, -jnp.inf, F32)
    l_scr[...] = jnp.zeros(l_scr.shape, F32)
    acc_scr[...] = jnp.zeros(acc_scr.shape, F32)
    p_scr[1] = jnp.zeros(p_scr.shape[1:], BF16)
    corr_scr[1] = jnp.ones(corr_scr.shape[1:], F32)
    scores(0, 0)

    def body(t, _):
        j = 2 * t
        scores(j + 1, 1)
        softmax(0)
        values(j - 1, 1)
        scores(j + 2, 0)
        softmax(1)
        values(j, 0)
        return 0

    lax.fori_loop(0, qi, body, 0)
    j = 2 * qi
    scores(j + 1, 1, row0=tk)
    softmax(0, col_offset=0)
    values(j - 1, 1)
    softmax(1, col_offset=tk, row0=tk)
    values(j, 0)
    values(j + 1, 1, row0=tk)
    o_ref[...] = acc_scr[...] / jnp.sum(l_scr[...], axis=-1, keepdims=True)


def _flash(q, k, v, *, n_heads, tq):
    s = q.shape[0]
    slot = q.shape[1] // n_heads
    vd = v.shape[1] // n_heads
    assert s % tq == 0
    tk = tq // 2
    return pl.pallas_call(
        functools.partial(_flash_kernel, tk=tk),
        grid=(n_heads, s // tq),
        in_specs=[pl.BlockSpec((tq, slot), lambda h, i: (i, h)),
                  pl.BlockSpec((s, slot), lambda h, i: (0, h)),
                  pl.BlockSpec((s, vd), lambda h, i: (0, h))],
        out_specs=pl.BlockSpec((tq, vd), lambda h, i: (i, h)),
        out_shape=jax.ShapeDtypeStruct((s, n_heads * vd), F32),
        scratch_shapes=[pltpu.VMEM((2, tq, tk), F32), pltpu.VMEM((2, tq, tk), BF16),
                        pltpu.VMEM((2, tq, LANES), F32), pltpu.VMEM((tq, LANES), F32),
                        pltpu.VMEM((tq, LANES), F32), pltpu.VMEM((tq, vd), F32)],
        compiler_params=_params(("parallel", "arbitrary")),
        name="flash_prompt",
    )(q, k, v)


PAGES_PER_BLOCK = 16
N_SLOTS = 6


def _paged_kernel(pt_ref, wukT_ref, qlat_ref, qpe_ref, cnew_ref, kpenew_ref, g1_ref, g2_ref, cos_ref,
                  sin_ref, cosn_ref, sinn_ref, ckv_hbm, kpe_hbm, o_ref, wcat, ckv_buf, kpe_buf, s_scr, sems,
                  *, layer, scale, qk_dim, page, n_steps):
    n_heads, t_new, rank = qlat_ref.shape
    rope_dim = qpe_ref.shape[2]
    half = rope_dim // 2
    hrows = wukT_ref.shape[0]
    nope = hrows // n_heads
    b = pl.program_id(0)
    total = pl.num_programs(0) * n_steps

    def copies(g, slot):
        out = []
        for p in range(PAGES_PER_BLOCK):
            pid = pt_ref[g * PAGES_PER_BLOCK + p]
            out.append(pltpu.make_async_copy(ckv_hbm.at[layer, pid],
                                             ckv_buf.at[slot, pl.ds(p * page, page), :],
                                             sems.at[slot, 2 * p]))
            out.append(pltpu.make_async_copy(kpe_hbm.at[layer, pid], kpe_buf.at[slot, p],
                                             sems.at[slot, 2 * p + 1]))
        return out

    def issue(g):
        nxt = g + (N_SLOTS - 1)
        for c in copies(jnp.minimum(nxt, total - 1), nxt % N_SLOTS):
            c.start()

    def wait(g):
        for c in copies(g, g % N_SLOTS):
            c.wait()

    g0 = b * n_steps

    @pl.when(b == 0)
    def _():
        wcat[0:hrows, :] = wukT_ref[...]
        for g in range(N_SLOTS - 1):
            for c in copies(g, g):
                c.start()

    wcat[hrows:, :] = qlat_ref[...].reshape(n_heads * t_new, rank).astype(BF16)
    qpe = qpe_ref[...].reshape(n_heads * t_new, rope_dim).astype(BF16)

    def scores(c_bf, kpe_t, g1, g2, cos, sin):
        kt = _dot_nt(wcat[...], c_bf)
        ssq_pe = jnp.sum(kpe_t * kpe_t, axis=0, keepdims=True)
        x1 = kpe_t[:half] * g1
        x2 = kpe_t[half:] * g2
        kr = jnp.concatenate([x1 * cos - x2 * sin, x2 * cos + x1 * sin], axis=0).astype(BF16)
        s_raw = kt[hrows:] + _dot(qpe, kr)
        parts = []
        for h in range(n_heads):
            kh = kt[h * nope:(h + 1) * nope]
            ssq = jnp.sum(kh * kh, axis=0, keepdims=True) + ssq_pe
            rinv = lax.rsqrt(ssq * (1.0 / qk_dim) + EPS) * (scale * LOG2_E)
            parts.append(s_raw[h * t_new:(h + 1) * t_new] * rinv)
        return jnp.concatenate(parts, axis=0)

    def page_scores(g, i):
        slot = g % N_SLOTS
        kpe_t = jnp.concatenate([kpe_buf[slot, p] for p in range(PAGES_PER_BLOCK)], axis=1)
        return scores(ckv_buf[slot].astype(BF16), kpe_t, g1_ref[...], g2_ref[...], cos_ref[i], sin_ref[i])

    def page_update(g, s, carry):
        return _softmax_update(s, ckv_buf[g % N_SLOTS].astype(BF16), carry)

    wait(g0)
    s_scr[...] = page_scores(g0, 0)

    rows = n_heads * t_new
    c_new = jnp.concatenate([cnew_ref[...], jnp.zeros((page - t_new, rank), F32)], axis=0).astype(BF16)
    s = scores(c_new, kpenew_ref[...], g1_ref[:, 0:page], g2_ref[:, 0:page], cosn_ref[...], sinn_ref[...])
    key = lax.broadcasted_iota(jnp.int32, s.shape, 1)
    qt = lax.broadcasted_iota(jnp.int32, s.shape, 0) % t_new
    carry = _softmax_update(jnp.where(key <= qt, s, -jnp.inf), c_new, _softmax_init(rows, rank))

    def step(i, carry):
        g = g0 + i
        wait(g + 1)
        s = s_scr[...]
        s_scr[...] = page_scores(g + 1, i + 1)
        issue(g)
        return page_update(g, s, carry)

    carry = lax.fori_loop(0, n_steps - 1, step, carry)
    g_last = g0 + n_steps - 1
    issue(g_last)
    o_ref[...] = _softmax_finish(page_update(g_last, s_scr[...], carry))

    @pl.when(b == pl.num_programs(0) - 1)
    def _():
        for k in range(1, N_SLOTS):
            for c in copies(total - 1, (total - 1 + k) % N_SLOTS):
                c.wait()


def _paged(qlat, qpe, ckv_new, kpe_new, page_table, cache_ckv, cache_kpe_t, lw, tabs, *, layer, scale):
    n_heads, n, rank = qlat.shape
    rope_dim = qpe.shape[2]
    db, n_pages = page_table.shape
    t_new = n // db
    page = cache_ckv.shape[2]
    assert t_new == SUBLANES and page == LANES and n_pages % PAGES_PER_BLOCK == 0
    n_steps = n_pages // PAGES_PER_BLOCK
    assert db * n_steps >= N_SLOTS
    kb = PAGES_PER_BLOCK * page
    half = rope_dim // 2
    hrows = lw["w_ukT"].shape[0]
    rows = n_heads * t_new
    cos_t, sin_t, cos_n, sin_n = tabs
    kpe_new_t = jnp.pad(kpe_new.reshape(db, t_new, rope_dim).transpose(0, 2, 1),
                        ((0, 0), (0, 0), (0, page - t_new)))
    grid_spec = pltpu.PrefetchScalarGridSpec(
        num_scalar_prefetch=1, grid=(db,),
        in_specs=[pl.BlockSpec((hrows, rank), lambda b, pt: (0, 0)),
                  pl.BlockSpec((n_heads, t_new, rank), lambda b, pt: (0, b, 0)),
                  pl.BlockSpec((n_heads, t_new, rope_dim), lambda b, pt: (0, b, 0)),
                  pl.BlockSpec((t_new, rank), lambda b, pt: (b, 0)),
                  pl.BlockSpec((None, rope_dim, page), lambda b, pt: (b, 0, 0)),
                  pl.BlockSpec((half, kb), lambda b, pt: (0, 0)),
                  pl.BlockSpec((half, kb), lambda b, pt: (0, 0)),
                  pl.BlockSpec((n_steps, half, kb), lambda b, pt: (0, 0, 0)),
                  pl.BlockSpec((n_steps, half, kb), lambda b, pt: (0, 0, 0)),
                  pl.BlockSpec((half, page), lambda b, pt: (0, 0)),
                  pl.BlockSpec((half, page), lambda b, pt: (0, 0)),
                  pl.BlockSpec(memory_space=pl.ANY),
                  pl.BlockSpec(memory_space=pl.ANY)],
        out_specs=pl.BlockSpec((None, rows, rank), lambda b, pt: (b, 0, 0)),
        scratch_shapes=[pltpu.VMEM((hrows + rows, rank), BF16),
                        pltpu.VMEM((N_SLOTS, kb, rank), F32),
                        pltpu.VMEM((N_SLOTS, PAGES_PER_BLOCK, rope_dim, page), F32),
                        pltpu.VMEM((rows, kb), F32),
                        pltpu.SemaphoreType.DMA((N_SLOTS, 2 * PAGES_PER_BLOCK))])
    return pl.pallas_call(
        functools.partial(_paged_kernel, layer=layer, scale=scale, qk_dim=lw["dims"][4] + rope_dim,
                          page=page, n_steps=n_steps),
        grid_spec=grid_spec,
        out_shape=jax.ShapeDtypeStruct((db, rows, rank), F32),
        compiler_params=_params(("arbitrary",)),
        name="paged_sample",
    )(page_table.reshape(-1), lw["w_ukT"], qlat, qpe, ckv_new, kpe_new_t, lw["gk_pe1"], lw["gk_pe2"], cos_t, sin_t,
      cos_n, sin_n, cache_ckv, cache_kpe_t)


def _uvproj_kernel(olat_ref, wuv_ref, o_ref):
    db, t_new, rank = olat_ref.shape
    o_ref[...] = _dot(olat_ref[...].reshape(db * t_new, rank).astype(BF16), wuv_ref[...])


def _uvproj(olat, w_uv_h, *, t_new):
    db, rows, rank = olat.shape
    n_heads, _, vd = w_uv_h.shape
    olat4 = olat.reshape(db, n_heads, t_new, rank)
    return pl.pallas_call(
        _uvproj_kernel,
        grid=(n_heads,),
        in_specs=[pl.BlockSpec((db, None, t_new, rank), lambda h: (0, h, 0, 0)),
                  pl.BlockSpec((None, rank, vd), lambda h: (h, 0, 0))],
        out_specs=pl.BlockSpec((db * t_new, vd), lambda h: (0, h)),
        out_shape=jax.ShapeDtypeStruct((db * t_new, n_heads * vd), F32),
        compiler_params=_params(("parallel",)),
        name="uvproj_sample",
    )(olat4, w_uv_h)


def _merge_kernel(x_ref, yr_ref, ya_ref, gattn_ref, wout_ref, gmlp_ref, h_ref, hn_ref):
    d_rnn = yr_ref.shape[1]
    yan = _rms(ya_ref[...], gattn_ref[...]).astype(BF16)
    h = x_ref[...] + _dot(yr_ref[...], wout_ref[0:d_rnn, :]) + _dot(yan, wout_ref[d_rnn:, :])
    h_ref[...] = h
    hn_ref[...] = _rms(h, gmlp_ref[...]).astype(BF16)


def _merge(x, y_rnn, y_attn, lw, *, tm):
    n, d_model = x.shape
    d_rnn, d_attn = y_rnn.shape[1], y_attn.shape[1]
    row = lambda w: pl.BlockSpec((tm, w), lambda i: (i, 0))
    return pl.pallas_call(
        _merge_kernel,
        grid=(n // tm,),
        in_specs=[row(d_model), row(d_rnn), row(d_attn), _const_spec((1, d_attn)),
                  _layer_spec(lw["w_out"].shape, lw["layer"]), _const_spec((1, d_model))],
        out_specs=[row(d_model), row(d_model)],
        out_shape=[jax.ShapeDtypeStruct((n, d_model), F32), jax.ShapeDtypeStruct((n, d_model), BF16)],
        compiler_params=_params(("parallel",)),
        name="merge",
    )(x, y_rnn, y_attn, lw["g_out_attn"], lw["w_out"], lw["g_mlp"])


def _mlp_kernel(h_ref, hn_ref, wup_ref, wdn_ref, o_ref):
    @pl.when(pl.program_id(1) == 0)
    def _():
        o_ref[...] = h_ref[...]
    up = jnp.maximum(_dot(hn_ref[...], wup_ref[...]), 0.0)
    o_ref[...] += _dot((up * up).astype(BF16), wdn_ref[...])


def _mlp(h, hn, lw, *, tm, tf):
    n, d_model = h.shape
    d_ff = lw["w_up"].shape[2]
    layer = lw["layer"]
    assert n % tm == 0 and d_ff % tf == 0
    return pl.pallas_call(
        _mlp_kernel,
        grid=(n // tm, d_ff // tf),
        in_specs=[pl.BlockSpec((tm, d_model), lambda i, f: (i, 0)),
                  pl.BlockSpec((tm, d_model), lambda i, f: (i, 0)),
                  pl.BlockSpec((None, d_model, tf), lambda i, f: (layer, 0, f)),
                  pl.BlockSpec((None, tf, d_model), lambda i, f: (layer, f, 0))],
        out_specs=pl.BlockSpec((tm, d_model), lambda i, f: (i, 0)),
        out_shape=jax.ShapeDtypeStruct((n, d_model), F32),
        compiler_params=_params(("parallel", "arbitrary")),
        name="mlp",
    )(h, hn, lw["w_up"], lw["w_down"])


def _rope_angles(pos, rope_dim):
    half = rope_dim // 2
    inv_freq = ROPE_THETA ** (-jnp.arange(half, dtype=F32) / half)
    return pos.astype(F32)[:, None] * inv_freq[None, :]


def _token_tables(pos, rope_dim):
    ang = _rope_angles(pos, rope_dim)
    cos, sin = jnp.cos(ang), jnp.sin(ang)
    pad = jnp.zeros((pos.shape[0], LANES - rope_dim), F32)
    return jnp.concatenate([cos, cos, pad], axis=1), jnp.concatenate([-sin, sin, pad], axis=1)


def _key_tables(pos, rope_dim, block):
    ang = _rope_angles(pos, rope_dim).T
    half = rope_dim // 2
    nb = pos.shape[0] // block
    to_blocks = lambda a: a.reshape(half, nb, block).transpose(1, 0, 2)
    return to_blocks(jnp.cos(ang)), to_blocks(jnp.sin(ang))


def _stacked_weights(w_in, w_out, w_up, w_down, rope_dim):
    depth, d_model, _ = w_in.shape
    w_in = jnp.concatenate([w_in, jnp.zeros((depth, d_model, LANES - rope_dim), F32)], axis=2)
    return dict(w_in=w_in.astype(BF16), w_out=w_out.astype(BF16), w_up=w_up.astype(BF16),
                w_down=w_down.astype(BF16))


def _layer_weights(l, stacked, w_qb, w_uk, w_uv, lru_wa, lru_wx, vecs):
    kv_rank, n_heads, nope = w_uk.shape[1:]
    q_rank = w_qb.shape[1]
    d_rnn = lru_wa.shape[1] * lru_wa.shape[2]
    qk_dim = w_qb.shape[2] // n_heads
    rope_dim = qk_dim - nope
    half = rope_dim // 2
    slot = 2 * LANES
    kb = PAGES_PER_BLOCK * LANES
    v = {k: a[l][None, :] for k, a in vecs.items()}
    g_qn, g_kn = v.pop("g_qn"), v.pop("g_kn")
    pad = jnp.zeros((1, LANES - rope_dim), F32)
    lw = dict(v)
    lw.update(stacked)
    lw["layer"] = l
    lw["dims"] = (d_rnn, q_rank, kv_rank, n_heads, nope, rope_dim)
    lw["gq0"], lw["gq1"] = g_qn[:, :nope], jnp.concatenate([g_qn[:, nope:], pad], axis=1)
    lw["gk0"], lw["gk1"] = g_kn[:, :nope], jnp.concatenate([g_kn[:, nope:], pad], axis=1)
    lw["gk_pe1"] = jnp.broadcast_to(g_kn[0, nope:nope + half][:, None], (half, kb))
    lw["gk_pe2"] = jnp.broadcast_to(g_kn[0, nope + half:][:, None], (half, kb))
    wq = w_qb[l].reshape(q_rank, n_heads, qk_dim)
    wq = jnp.concatenate([wq, jnp.zeros((q_rank, n_heads, slot - qk_dim), F32)], axis=2)
    lw["w_qb"] = wq.reshape(q_rank, n_heads * slot).astype(BF16)
    lw["w_uk"] = w_uk[l].reshape(kv_rank, n_heads * nope).astype(BF16)
    lw["w_ukT_h"] = w_uk[l].transpose(1, 2, 0).astype(BF16)
    lw["w_ukT"] = lw["w_ukT_h"].reshape(n_heads * nope, kv_rank)
    lw["w_uv"] = w_uv[l].reshape(kv_rank, -1).astype(BF16)
    lw["w_uv_h"] = w_uv[l].transpose(1, 0, 2).astype(BF16)
    lw["w_ax"] = jnp.concatenate([lru_wa[l], lru_wx[l]], axis=-1).astype(BF16)
    return lw


def _trunk_layer(x_p, x_s, state_conv8, h0_s, page_table, cache_ckv, cache_kpe, lw, tabs, layer, cfg):
    d_rnn, q_rank, kv_rank, n_heads, nope, rope_dim = lw["dims"]
    scale = (nope + rope_dim) ** -0.5
    db = page_table.shape[0]
    t_new = x_s.shape[0] // db

    xr, gate, ckv_p, kpe_p, q, k, v = _prep(x_p, lw, tabs["cos_p"], tabs["sin_p"], prompt=True, tm=cfg["tm"])
    y_rnn, hl_p = _rnn(xr, gate, None, None, lw, chain=True, tt=cfg["tt"])
    y_attn = _flash(q, k, v, n_heads=n_heads, tq=cfg["tq"])
    h, hn = _merge(x_p, y_rnn, y_attn, lw, tm=cfg["tm"])
    y_p = _mlp(h, hn, lw, tm=cfg["tm_mlp"], tf=cfg["tf"])
    conv_p = xr[-(CONV_W - 1):]

    xr, gate, ckv_s, kpe_s, qlat, qpe = _prep(x_s, lw, tabs["cos_s"], tabs["sin_s"], prompt=False,
                                              tm=cfg["tm"])
    y_rnn, hl_s = _rnn(xr, gate, state_conv8, h0_s, lw, chain=False, tt=cfg["tt"])
    olat = _paged(qlat, qpe, ckv_s, kpe_s, page_table, cache_ckv, cache_kpe, lw,
                  (tabs["cos_k"], tabs["sin_k"], tabs["cos_n"], tabs["sin_n"]), layer=layer, scale=scale)
    y_attn = _uvproj(olat, lw["w_uv_h"], t_new=t_new)
    h, hn = _merge(x_s, y_rnn, y_attn, lw, tm=cfg["tm"])
    y_s = _mlp(h, hn, lw, tm=cfg["tm_mlp"], tf=cfg["tf"])
    conv_s = xr.reshape(db, t_new, d_rnn)[:, t_new - (CONV_W - 1):]
    return (y_p, y_s, ckv_p, kpe_p, hl_p[-1], conv_p, ckv_s.reshape(db, t_new, kv_rank),
            kpe_s.reshape(db, t_new, rope_dim), hl_s[:, 0], conv_s)


def kernel(x_prompt, x_sample, cache_ckv, cache_kpe, state_rglru_h, state_conv, page_table, g_mix, w_in,
           conv_w, conv_b, lru_wa, lru_ba, lru_wx, lru_bx, lru_lambda, g_q_a, w_qb, g_kv_a, w_uk, w_uv,
           g_qn, g_kn, g_out_rnn, g_out_attn, w_out, g_mlp, w_up, w_down):
    batch, seq, d_model = x_prompt.shape
    db, t_new, _ = x_sample.shape
    depth = w_in.shape[0]
    assert batch == 1
    n_pages = page_table.shape[1]
    page = cache_ckv.shape[2]
    past = n_pages * page
    rope_dim = cache_kpe.shape[3]
    d_rnn = state_rglru_h.shape[2]
    kb = PAGES_PER_BLOCK * page

    cos_p, sin_p = _token_tables(jnp.arange(seq, dtype=jnp.int32), rope_dim)
    pos_s = past + jnp.arange(t_new, dtype=jnp.int32)
    cos_s, sin_s = _token_tables(jnp.tile(pos_s, db), rope_dim)
    cos_k, sin_k = _key_tables(jnp.arange(past, dtype=jnp.int32), rope_dim, kb)
    cos_n, sin_n = _key_tables(past + jnp.arange(page, dtype=jnp.int32), rope_dim, page)
    tabs = dict(cos_p=cos_p, sin_p=sin_p, cos_s=cos_s, sin_s=sin_s, cos_k=cos_k, sin_k=sin_k,
                cos_n=cos_n[0], sin_n=sin_n[0])
    cfg = dict(tm=min(512, seq, db * t_new), tt=min(32, db * t_new // SUBLANES), tq=min(1024, seq),
               tm_mlp=min(512, seq, db * t_new), tf=min(1024, w_up.shape[2]))

    vecs = dict(g_mix=g_mix, conv_b=conv_b, lru_ba=lru_ba.reshape(depth, d_rnn),
                lru_bx=lru_bx.reshape(depth, d_rnn), lru_lambda=lru_lambda, g_q_a=g_q_a, g_kv_a=g_kv_a,
                g_qn=g_qn, g_kn=g_kn, g_out_rnn=g_out_rnn, g_out_attn=g_out_attn, g_mlp=g_mlp)
    y_p = x_prompt.reshape(seq, d_model)
    y_s = x_sample.reshape(db * t_new, d_model)
    state_conv8 = jnp.pad(state_conv, ((0, 0), (0, 0), (SUBLANES - (CONV_W - 1), 0), (0, 0)))
    cache_kpe = cache_kpe.transpose(0, 1, 3, 2)
    stacked = _stacked_weights(w_in, w_out, w_up, w_down, rope_dim)
    outs = []
    for l in range(depth):
        lw = _layer_weights(l, stacked, w_qb, w_uk, w_uv, lru_wa, lru_wx, vecs)
        lw["conv_w"] = conv_w[l]
        res = _trunk_layer(y_p, y_s, state_conv8[l], state_rglru_h[l][:, None, :], page_table, cache_ckv,
                           cache_kpe, lw, tabs, l, cfg)
        y_p, y_s = res[0], res[1]
        outs.append(res[2:])
    ckv_p, kpe_p, h_p, conv_p, ckv_s, kpe_s, h_s, conv_s = (jnp.stack(a) for a in zip(*outs))
    return (y_p.reshape(batch, seq, d_model), y_s.reshape(db, t_new, d_model),
            ckv_p[:, None], kpe_p[:, None], h_p, conv_p[:, None], ckv_s, kpe_s, h_s, conv_s)
```

```python
import functools
import math

import jax
import jax.numpy as jnp
from jax import lax
from jax.experimental import pallas as pl
from jax.experimental.pallas import tpu as pltpu

EPS = 1e-6
ROPE_THETA = 10000.0
LRU_C = 8.0
CONV_W = 4
LOG2_E = math.log2(math.e)

SUBLANES = 8
LANES = 128
MXU_DIM = 256
VMEM_LIMIT_BYTES = 56 * 1024 * 1024

BF16 = jnp.bfloat16
F32 = jnp.float32

_NT_DIMS = (((1,), (1,)), ((), ()))


def _dot(a, b):
    return jnp.dot(a, b, preferred_element_type=F32)


def _dot_nt(a, b):
    return lax.dot_general(a, b, _NT_DIMS, preferred_element_type=F32)


def _rms(x, g):
    return x * lax.rsqrt(jnp.mean(x * x, axis=-1, keepdims=True) + EPS) * g


def _const_spec(shape):
    zeros = (0,) * len(shape)
    return pl.BlockSpec(shape, lambda *_: zeros)


def _layer_spec(stacked_shape, layer):
    return pl.BlockSpec((None,) + tuple(stacked_shape[1:]), lambda *_: (layer, 0, 0))


def _params(semantics):
    return pltpu.CompilerParams(dimension_semantics=semantics, vmem_limit_bytes=VMEM_LIMIT_BYTES)


def _head_slot(v0, v1, g0, g1, cos, sin, qk_dim):
    ssq = jnp.sum(v0 * v0, axis=-1, keepdims=True) + jnp.sum(v1 * v1, axis=-1, keepdims=True)
    rinv = lax.rsqrt(ssq * (1.0 / qk_dim) + EPS)
    n0 = v0 * rinv * g0
    n1 = v1 * rinv * g1
    return n0, _rope_slot(n1, cos, sin)


def _rope_slot(n1, cos, sin):
    partner = pltpu.roll(n1, 32, 1) + pltpu.roll(n1, 96, 1)
    return n1 * cos + partner * sin


def _prep_kernel(x_ref, gmix_ref, win_ref, gqa_ref, wqb_ref, gkva_ref, gq0_ref, gq1_ref, gk0_ref,
                 gk1_ref, cos_ref, sin_ref, wk_ref, wv_ref, *out_refs, dims, prompt):
    d_rnn, q_rank, kv_rank, n_heads, nope, rope_dim = dims
    qk_dim = nope + rope_dim
    qscale = qk_dim ** -0.5 * LOG2_E
    slot = 2 * LANES
    if prompt:
        xr_ref, gate_ref, ckv_ref, kpe_ref, q_ref, k_ref, v_ref = out_refs
    else:
        xr_ref, gate_ref, ckv_ref, kpe_ref, qlat_ref, qpe_ref = out_refs

    xn = _rms(x_ref[...], gmix_ref[...]).astype(BF16)
    o = 0
    xr_ref[...] = _dot(xn, win_ref[:, o:o + d_rnn]); o += d_rnn
    gate_ref[...] = _dot(xn, win_ref[:, o:o + d_rnn]); o += d_rnn
    qa = _dot(xn, win_ref[:, o:o + q_rank]); o += q_rank
    kvc = _dot(xn, win_ref[:, o:o + kv_rank]); o += kv_rank
    kpe = _dot(xn, win_ref[:, o:o + LANES])

    cos = cos_ref[...]
    sin = sin_ref[...]
    ckv = _rms(kvc, gkva_ref[...])
    ckv_ref[...] = ckv
    kpe_ref[...] = kpe[:, :rope_dim]

    qf = _dot(_rms(qa, gqa_ref[...]).astype(BF16), wqb_ref[...])
    gq0, gq1 = gq0_ref[...], gq1_ref[...]
    gk0, gk1 = gk0_ref[...], gk1_ref[...]
    for h in range(n_heads):
        v0 = qf[:, h * slot:h * slot + LANES]
        v1 = qf[:, h * slot + LANES:(h + 1) * slot]
        n0, r1 = _head_slot(v0, v1, gq0, gq1, cos, sin, qk_dim)
        if prompt:
            q_ref[:, h * slot:h * slot + LANES] = (n0 * qscale).astype(BF16)
            q_ref[:, h * slot + LANES:(h + 1) * slot] = (r1 * qscale).astype(BF16)
        else:
            qlat_ref[h] = _dot((n0 * gk0).astype(BF16), wk_ref[h])
            qpe_ref[h] = r1[:, :rope_dim]

    if prompt:
        ckv_bf = ckv.astype(BF16)
        v_ref[...] = _dot(ckv_bf, wv_ref[...]).astype(BF16)
        kn = _dot(ckv_bf, wk_ref[...])
        ssq_pe = jnp.sum(kpe * kpe, axis=-1, keepdims=True)
        kr = _rope_slot(kpe * gk1, cos, sin)
        for h in range(n_heads):
            kh = kn[:, h * nope:(h + 1) * nope]
            ssq = jnp.sum(kh * kh, axis=-1, keepdims=True) + ssq_pe
            rinv = lax.rsqrt(ssq * (1.0 / qk_dim) + EPS)
            k_ref[:, h * slot:h * slot + LANES] = (kh * rinv * gk0).astype(BF16)
            k_ref[:, h * slot + LANES:(h + 1) * slot] = (kr * rinv).astype(BF16)


def _prep(x, lw, cos, sin, *, prompt, tm):
    n, d_model = x.shape
    dims = lw["dims"]
    d_rnn, q_rank, kv_rank, n_heads, nope, rope_dim = dims
    slot = 2 * LANES
    assert nope == LANES and rope_dim == LANES // 2 and n % tm == 0
    row = lambda w: pl.BlockSpec((tm, w), lambda i: (i, 0))
    wk = lw["w_uk"] if prompt else lw["w_ukT_h"]
    in_specs = [row(d_model), _const_spec((1, d_model)), _layer_spec(lw["w_in"].shape, lw["layer"]),
                _const_spec((1, q_rank)), _const_spec(lw["w_qb"].shape), _const_spec((1, kv_rank)),
                _const_spec((1, LANES)), _const_spec((1, LANES)), _const_spec((1, LANES)),
                _const_spec((1, LANES)), row(LANES), row(LANES), _const_spec(wk.shape),
                _const_spec(lw["w_uv"].shape)]
    out_shape = [jax.ShapeDtypeStruct((n, d_rnn), F32), jax.ShapeDtypeStruct((n, d_rnn), F32),
                 jax.ShapeDtypeStruct((n, kv_rank), F32), jax.ShapeDtypeStruct((n, rope_dim), F32)]
    out_specs = [row(d_rnn), row(d_rnn), row(kv_rank), row(rope_dim)]
    if prompt:
        out_shape += [jax.ShapeDtypeStruct((n, n_heads * slot), BF16),
                      jax.ShapeDtypeStruct((n, n_heads * slot), BF16),
                      jax.ShapeDtypeStruct((n, n_heads * nope), BF16)]
        out_specs += [row(n_heads * slot), row(n_heads * slot), row(n_heads * nope)]
    else:
        out_shape += [jax.ShapeDtypeStruct((n_heads, n, kv_rank), F32),
                      jax.ShapeDtypeStruct((n_heads, n, rope_dim), F32)]
        out_specs += [pl.BlockSpec((n_heads, tm, kv_rank), lambda i: (0, i, 0)),
                      pl.BlockSpec((n_heads, tm, rope_dim), lambda i: (0, i, 0))]
    return pl.pallas_call(
        functools.partial(_prep_kernel, dims=dims, prompt=prompt),
        grid=(n // tm,), in_specs=in_specs, out_specs=out_specs, out_shape=out_shape,
        compiler_params=_params(("parallel",)),
        name="prep_prompt" if prompt else "prep_sample",
    )(x, lw["g_mix"], lw["w_in"], lw["g_q_a"], lw["w_qb"], lw["g_kv_a"], lw["gq0"], lw["gq1"],
      lw["gk0"], lw["gk1"], cos, sin, wk, lw["w_uv"])


def _gelu_tanh(x):
    c = math.sqrt(2.0 / math.pi)
    return 0.5 * x * (1.0 + jnp.tanh(c * (x + 0.044715 * (x * x * x))))


def _rnn_kernel(x_ref, gate_ref, prev_ref, h0_ref, cw_ref, cb_ref, wax_ref, ba_ref, bx_ref, lam_ref,
                gout_ref, y_ref, hlast_ref, prev_scr, carry_scr, y_scr, *, chain, n_blocks):
    tt, _, c = x_ref.shape
    blk = c // n_blocks
    x = x_ref[...]
    if chain:
        @pl.when(pl.program_id(0) == 0)
        def _():
            prev_scr[...] = jnp.zeros_like(prev_scr)
            carry_scr[...] = jnp.zeros_like(carry_scr)
        xprev = jnp.concatenate([prev_scr[...], x[:-1]], axis=0)
        prev_scr[...] = x[-1:]
    else:
        xprev = prev_ref[...]

    t_idx = lax.broadcasted_iota(jnp.int32, (tt, SUBLANES, blk), 1)
    softplus_neg_lam = jnp.logaddexp(-lam_ref[...], 0.0)
    ysq = jnp.zeros((tt, SUBLANES, 1), F32)
    for k in range(n_blocks):
        cs = slice(k * blk, (k + 1) * blk)
        xk, xpk = x[:, :, cs], xprev[:, :, cs]
        u = cb_ref[:, cs] + cw_ref[CONV_W - 1:CONV_W, cs] * xk
        for s in range(1, CONV_W):
            sh = jnp.where(t_idx >= s, pltpu.roll(xk, s, 1), pltpu.roll(xpk, s, 1))
            u = u + cw_ref[CONV_W - 1 - s:CONV_W - s, cs] * sh
        u2 = u.reshape(tt * SUBLANES, blk)
        gates = _dot(u2.astype(BF16), wax_ref[k])
        r = jax.nn.sigmoid(gates[:, :blk] + ba_ref[:, cs])
        i = jax.nn.sigmoid(gates[:, blk:] + bx_ref[:, cs])
        log_a = (-LRU_C * r) * softplus_neg_lam[:, cs]
        a = jnp.exp(log_a)
        one_minus_a2 = -jnp.tanh(log_a) * (a * a + 1.0)
        b = (jnp.sqrt(one_minus_a2) * (i * u2)).reshape(tt, SUBLANES, blk)
        a = a.reshape(tt, SUBLANES, blk)
        for s in (1, 2, 4):
            m = t_idx >= s
            a_sh = jnp.where(m, pltpu.roll(a, s, 1), 1.0)
            b_sh = jnp.where(m, pltpu.roll(b, s, 1), 0.0)
            b = a * b_sh + b
            a = a * a_sh
        if chain:
            carry = carry_scr[:, cs]
            carries = []
            for j in range(tt):
                carries.append(carry)
                carry = a[j, SUBLANES - 1:, :] * carry + b[j, SUBLANES - 1:, :]
            carry_scr[:, cs] = carry
            hlast_ref[:, cs] = carry
            h_in = jnp.stack(carries, axis=0)
        else:
            h_in = h0_ref[:, :, cs]
        h = a * h_in + b
        if not chain:
            hlast_ref[:, :, cs] = h[:, SUBLANES - 1:, :]
        y = h * _gelu_tanh(gate_ref[:, :, cs])
        ysq = ysq + jnp.sum(y * y, axis=-1, keepdims=True)
        y_scr[:, :, cs] = y
    rinv = lax.rsqrt(ysq * (1.0 / c) + EPS)
    y_ref[...] = (y_scr[...] * rinv * gout_ref[...]).reshape(tt * SUBLANES, c).astype(BF16)


def _rnn(xr, gate, prev8, h0, lw, *, chain, tt):
    n, c = xr.shape
    nt = n // SUBLANES
    assert nt % tt == 0
    n_blocks = lw["w_ax"].shape[0]
    x3 = xr.reshape(nt, SUBLANES, c)
    g3 = gate.reshape(nt, SUBLANES, c)
    tile = pl.BlockSpec((tt, SUBLANES, c), lambda i: (i, 0, 0))
    if chain:
        prev8 = jnp.zeros((1, SUBLANES, c), F32)
        h0 = jnp.zeros((1, 1, c), F32)
        prev_spec, h0_spec = _const_spec((1, SUBLANES, c)), _const_spec((1, 1, c))
        hl_shape = jax.ShapeDtypeStruct((nt // tt, 1, c), F32)
        hl_spec = pl.BlockSpec((None, 1, c), lambda i: (i, 0, 0))
    else:
        prev_spec = tile
        h0_spec = pl.BlockSpec((tt, 1, c), lambda i: (i, 0, 0))
        hl_shape = jax.ShapeDtypeStruct((nt, 1, c), F32)
        hl_spec = pl.BlockSpec((tt, 1, c), lambda i: (i, 0, 0))
    y, hlast = pl.pallas_call(
        functools.partial(_rnn_kernel, chain=chain, n_blocks=n_blocks),
        grid=(nt // tt,),
        in_specs=[tile, tile, prev_spec, h0_spec, _const_spec((CONV_W, c)), _const_spec((1, c)),
                  _const_spec(lw["w_ax"].shape), _const_spec((1, c)), _const_spec((1, c)),
                  _const_spec((1, c)), _const_spec((1, c))],
        out_specs=[pl.BlockSpec((tt * SUBLANES, c), lambda i: (i, 0)), hl_spec],
        out_shape=[jax.ShapeDtypeStruct((n, c), BF16), hl_shape],
        scratch_shapes=[pltpu.VMEM((1, SUBLANES, c), F32), pltpu.VMEM((1, c), F32),
                        pltpu.VMEM((tt, SUBLANES, c), F32)],
        compiler_params=_params(("arbitrary",)),
        name="rnn_chain" if chain else "rnn_tiles",
    )(x3, g3, prev8, h0, lw["conv_w"], lw["conv_b"], lw["w_ax"], lw["lru_ba"], lw["lru_bx"],
      lw["lru_lambda"], lw["g_out_rnn"])
    return y, hlast


def _lane_partial_sum(p):
    out = p[:, 0:LANES]
    for c in range(1, p.shape[1] // LANES):
        out = out + p[:, c * LANES:(c + 1) * LANES]
    return out


def _softmax_update(s, values, carry):
    m, l, acc = carry
    m_new = jnp.maximum(m, jnp.max(s, axis=-1, keepdims=True))
    p = jnp.exp2(s - m_new)
    corr = jnp.exp2(m - m_new)
    l = l * corr + _lane_partial_sum(p)
    acc = acc * corr + _dot(p.astype(BF16), values)
    return m_new, l, acc


def _softmax_init(rows, width):
    return (jnp.full((rows, 1), -jnp.inf, F32), jnp.zeros((rows, LANES), F32), jnp.zeros((rows, width), F32))


def _softmax_finish(carry):
    _, l, acc = carry
    return acc / jnp.sum(l, axis=-1, keepdims=True)


FLASH_ROW_CHUNK = 64


def _flash_kernel(q_ref, k_ref, v_ref, o_ref, s_scr, p_scr, corr_scr, m_scr, l_scr, acc_scr, *, tk):
    tq = q_ref.shape[0]
    assert tq == 2 * tk
    qi = pl.program_id(1)

    def rows_of(j):
        return pl.ds(pl.multiple_of(j * tk, tk), tk)

    def scores(j, slot, row0=0):
        s_scr[slot, row0:, :] = _dot_nt(q_ref[row0:, :], k_ref[rows_of(j), :])

    def softmax(slot, col_offset=None, row0=0):
        for r in range(row0, tq, FLASH_ROW_CHUNK):
            rs = slice(r, r + FLASH_ROW_CHUNK)
            s = s_scr[slot, rs, :]
            if col_offset is not None and r < col_offset + tk - 1:
                row = r + lax.broadcasted_iota(jnp.int32, s.shape, 0)
                col = col_offset + lax.broadcasted_iota(jnp.int32, s.shape, 1)
                s = jnp.where(col <= row, s, -jnp.inf)
            m = m_scr[rs, :]
            m_new = jnp.maximum(m, jnp.max(s, axis=-1, keepdims=True))
            corr = jnp.exp2(m - m_new)
            m_scr[rs, :] = m_new
            corr_scr[slot, rs, :] = corr
            l = l_scr[rs, :] * corr
            for c in range(0, tk, LANES):
                p = jnp.exp2(s[:, c:c + LANES] - m_new)
                l = l + p
                p_scr[slot, rs, c:c + LANES] = p.astype(BF16)
            l_scr[rs, :] = l

    def values(j, slot, row0=0):
        v = v_ref[rows_of(jnp.maximum(j, 0)), :]
        acc_scr[row0:, :] = acc_scr[row0:, :] * corr_scr[slot, row0:, :] + _dot(p_scr[slot, row0:, :], v)

    m_scr[...] = jnp.full(m_scr.shape, -jnp.inf, F32)
    l_scr[...] = jnp.zeros(l_scr.shape, F32)
    acc_scr[...] = jnp.zeros(acc_scr.shape, F32)
    p_scr[1] = jnp.zeros(p_scr.shape[1:], BF16)
    corr_scr[1] = jnp.ones(corr_scr.shape[1:], F32)
    scores(0, 0)

    def body(t, _):
        j = 2 * t
        scores(j + 1, 1)
        softmax(0)
        values(j - 1, 1)
        scores(j + 2, 0)
        softmax(1)
        values(j, 0)
        return 0

    lax.fori_loop(0, qi, body, 0)
    j = 2 * qi
    scores(j + 1, 1, row0=tk)
    softmax(0, col_offset=0)
    values(j - 1, 1)
    softmax(1, col_offset=tk, row0=tk)
    values(j, 0)
    values(j + 1, 1, row0=tk)
    o_ref[...] = acc_scr[...] / jnp.sum(l_scr[...], axis=-1, keepdims=True)


def _flash(q, k, v, *, n_heads, tq):
    s = q.shape[0]
    slot = q.shape[1] // n_heads
    vd = v.shape[1] // n_heads
    assert s % tq == 0
    tk = tq // 2
    return pl.pallas_call(
        functools.partial(_flash_kernel, tk=tk),
        grid=(n_heads, s // tq),
        in_specs=[pl.BlockSpec((tq, slot), lambda h, i: (i, h)),
                  pl.BlockSpec((s, slot), lambda h, i: (0, h)),
                  pl.BlockSpec((s, vd), lambda h, i: (0, h))],
        out_specs=pl.BlockSpec((tq, vd), lambda h, i: (i, h)),
        out_shape=jax.ShapeDtypeStruct((s, n_heads * vd), F32),
        scratch_shapes=[pltpu.VMEM((2, tq, tk), F32), pltpu.VMEM((2, tq, tk), BF16),
                        pltpu.VMEM((2, tq, LANES), F32), pltpu.VMEM((tq, LANES), F32),
                        pltpu.VMEM((tq, LANES), F32), pltpu.VMEM((tq, vd), F32)],
        compiler_params=_params(("parallel", "arbitrary")),
        name="flash_prompt",
    )(q, k, v)


PAGES_PER_BLOCK = 16
N_SLOTS = 6


def _paged_kernel(pt_ref, wukT_ref, qlat_ref, qpe_ref, cnew_ref, kpenew_ref, g1_ref, g2_ref, cos_ref,
                  sin_ref, cosn_ref, sinn_ref, ckv_hbm, kpe_hbm, o_ref, wcat, ckv_buf, kpe_buf, s_scr, sems,
                  *, layer, scale, qk_dim, page, n_steps):
    n_heads, t_new, rank = qlat_ref.shape
    rope_dim = qpe_ref.shape[2]
    half = rope_dim // 2
    hrows = wukT_ref.shape[0]
    nope = hrows // n_heads
    b = pl.program_id(0)
    total = pl.num_programs(0) * n_steps

    def copies(g, slot):
        latent, rope = [], []
        for p in range(PAGES_PER_BLOCK):
            pid = pt_ref[g * PAGES_PER_BLOCK + p]
            latent.append(pltpu.make_async_copy(ckv_hbm.at[layer, pid],
                                                ckv_buf.at[slot, pl.ds(p * page, page), :],
                                                sems.at[slot, 0]))
            rope.append(pltpu.make_async_copy(kpe_hbm.at[layer, pid], kpe_buf.at[slot, p],
                                              sems.at[slot, 1]))
        return latent + rope

    def issue(g):
        nxt = g + (N_SLOTS - 1)
        for c in copies(jnp.minimum(nxt, total - 1), nxt % N_SLOTS):
            c.start()

    def wait(g):
        for c in copies(g, g % N_SLOTS):
            c.wait()

    g0 = b * n_steps

    @pl.when(b == 0)
    def _():
        wcat[0:hrows, :] = wukT_ref[...]
        for g in range(N_SLOTS - 1):
            for c in copies(g, g):
                c.start()

    wcat[hrows:, :] = qlat_ref[...].reshape(n_heads * t_new, rank).astype(BF16)
    qpe = qpe_ref[...].reshape(n_heads * t_new, rope_dim).astype(BF16)

    def scores(c_bf, kpe_t, g1, g2, cos, sin):
        kt = _dot_nt(wcat[...], c_bf)
        ssq_pe = jnp.sum(kpe_t * kpe_t, axis=0, keepdims=True)
        x1 = kpe_t[:half] * g1
        x2 = kpe_t[half:] * g2
        kr = jnp.concatenate([x1 * cos - x2 * sin, x2 * cos + x1 * sin], axis=0).astype(BF16)
        s_raw = kt[hrows:] + _dot(qpe, kr)
        parts = []
        for h in range(n_heads):
            kh = kt[h * nope:(h + 1) * nope]
            ssq = jnp.sum(kh * kh, axis=0, keepdims=True) + ssq_pe
            rinv = lax.rsqrt(ssq * (1.0 / qk_dim) + EPS) * (scale * LOG2_E)
            parts.append(s_raw[h * t_new:(h + 1) * t_new] * rinv)
        return jnp.concatenate(parts, axis=0)

    def page_scores(g, i):
        slot = g % N_SLOTS
        kpe_t = jnp.concatenate([kpe_buf[slot, p] for p in range(PAGES_PER_BLOCK)], axis=1)
        return scores(ckv_buf[slot].astype(BF16), kpe_t, g1_ref[...], g2_ref[...], cos_ref[i], sin_ref[i])

    def page_update(g, s, carry):
        return _softmax_update(s, ckv_buf[g % N_SLOTS].astype(BF16), carry)

    wait(g0)
    s_scr[...] = page_scores(g0, 0)

    rows = n_heads * t_new
    c_new = jnp.concatenate([cnew_ref[...], jnp.zeros((page - t_new, rank), F32)], axis=0).astype(BF16)
    s = scores(c_new, kpenew_ref[...], g1_ref[:, 0:page], g2_ref[:, 0:page], cosn_ref[...], sinn_ref[...])
    key = lax.broadcasted_iota(jnp.int32, s.shape, 1)
    qt = lax.broadcasted_iota(jnp.int32, s.shape, 0) % t_new
    carry = _softmax_update(jnp.where(key <= qt, s, -jnp.inf), c_new, _softmax_init(rows, rank))

    def step(i, carry):
        g = g0 + i
        wait(g + 1)
        s = s_scr[...]
        s_scr[...] = page_scores(g + 1, i + 1)
        issue(g)
        return page_update(g, s, carry)

    carry = lax.fori_loop(0, n_steps - 1, step, carry)
    g_last = g0 + n_steps - 1
    issue(g_last)
    o_ref[...] = _softmax_finish(page_update(g_last, s_scr[...], carry))

    @pl.when(b == pl.num_programs(0) - 1)
    def _():
        for k in range(1, N_SLOTS):
            for c in copies(total - 1, (total - 1 + k) % N_SLOTS):
                c.wait()


def _paged(qlat, qpe, ckv_new, kpe_new, page_table, cache_ckv, cache_kpe_t, lw, tabs, *, layer, scale):
    n_heads, n, rank = qlat.shape
    rope_dim = qpe.shape[2]
    db, n_pages = page_table.shape
    t_new = n // db
    page = cache_ckv.shape[2]
    assert t_new == SUBLANES and page == LANES and n_pages % PAGES_PER_BLOCK == 0
    n_steps = n_pages // PAGES_PER_BLOCK
    assert db * n_steps >= N_SLOTS
    kb = PAGES_PER_BLOCK * page
    half = rope_dim // 2
    hrows = lw["w_ukT"].shape[0]
    rows = n_heads * t_new
    cos_t, sin_t, cos_n, sin_n = tabs
    kpe_new_t = jnp.pad(kpe_new.reshape(db, t_new, rope_dim).transpose(0, 2, 1),
                        ((0, 0), (0, 0), (0, page - t_new)))
    grid_spec = pltpu.PrefetchScalarGridSpec(
        num_scalar_prefetch=1, grid=(db,),
        in_specs=[pl.BlockSpec((hrows, rank), lambda b, pt: (0, 0)),
                  pl.BlockSpec((n_heads, t_new, rank), lambda b, pt: (0, b, 0)),
                  pl.BlockSpec((n_heads, t_new, rope_dim), lambda b, pt: (0, b, 0)),
                  pl.BlockSpec((t_new, rank), lambda b, pt: (b, 0)),
                  pl.BlockSpec((None, rope_dim, page), lambda b, pt: (b, 0, 0)),
                  pl.BlockSpec((half, kb), lambda b, pt: (0, 0)),
                  pl.BlockSpec((half, kb), lambda b, pt: (0, 0)),
                  pl.BlockSpec((n_steps, half, kb), lambda b, pt: (0, 0, 0)),
                  pl.BlockSpec((n_steps, half, kb), lambda b, pt: (0, 0, 0)),
                  pl.BlockSpec((half, page), lambda b, pt: (0, 0)),
                  pl.BlockSpec((half, page), lambda b, pt: (0, 0)),
                  pl.BlockSpec(memory_space=pl.ANY),
                  pl.BlockSpec(memory_space=pl.ANY)],
        out_specs=pl.BlockSpec((None, rows, rank), lambda b, pt: (b, 0, 0)),
        scratch_shapes=[pltpu.VMEM((hrows + rows, rank), BF16),
                        pltpu.VMEM((N_SLOTS, kb, rank), F32),
                        pltpu.VMEM((N_SLOTS, PAGES_PER_BLOCK, rope_dim, page), F32),
                        pltpu.VMEM((rows, kb), F32),
                        pltpu.SemaphoreType.DMA((N_SLOTS, 2))])
    return pl.pallas_call(
        functools.partial(_paged_kernel, layer=layer, scale=scale, qk_dim=lw["dims"][4] + rope_dim,
                          page=page, n_steps=n_steps),
        grid_spec=grid_spec,
        out_shape=jax.ShapeDtypeStruct((db, rows, rank), F32),
        compiler_params=_params(("arbitrary",)),
        name="paged_sample",
    )(page_table.reshape(-1), lw["w_ukT"], qlat, qpe, ckv_new, kpe_new_t, lw["gk_pe1"], lw["gk_pe2"], cos_t, sin_t,
      cos_n, sin_n, cache_ckv, cache_kpe_t)


def _uvproj_kernel(olat_ref, wuv_ref, o_ref):
    db, t_new, rank = olat_ref.shape
    o_ref[...] = _dot(olat_ref[...].reshape(db * t_new, rank).astype(BF16), wuv_ref[...])


def _uvproj(olat, w_uv_h, *, t_new):
    db, rows, rank = olat.shape
    n_heads, _, vd = w_uv_h.shape
    olat4 = olat.reshape(db, n_heads, t_new, rank)
    return pl.pallas_call(
        _uvproj_kernel,
        grid=(n_heads,),
        in_specs=[pl.BlockSpec((db, None, t_new, rank), lambda h: (0, h, 0, 0)),
                  pl.BlockSpec((None, rank, vd), lambda h: (h, 0, 0))],
        out_specs=pl.BlockSpec((db * t_new, vd), lambda h: (0, h)),
        out_shape=jax.ShapeDtypeStruct((db * t_new, n_heads * vd), F32),
        compiler_params=_params(("parallel",)),
        name="uvproj_sample",
    )(olat4, w_uv_h)


def _merge_kernel(x_ref, yr_ref, ya_ref, gattn_ref, wout_ref, gmlp_ref, h_ref, hn_ref):
    d_rnn = yr_ref.shape[1]
    yan = _rms(ya_ref[...], gattn_ref[...]).astype(BF16)
    h = x_ref[...] + _dot(yr_ref[...], wout_ref[0:d_rnn, :]) + _dot(yan, wout_ref[d_rnn:, :])
    h_ref[...] = h
    hn_ref[...] = _rms(h, gmlp_ref[...]).astype(BF16)


def _merge(x, y_rnn, y_attn, lw, *, tm):
    n, d_model = x.shape
    d_rnn, d_attn = y_rnn.shape[1], y_attn.shape[1]
    row = lambda w: pl.BlockSpec((tm, w), lambda i: (i, 0))
    return pl.pallas_call(
        _merge_kernel,
        grid=(n // tm,),
        in_specs=[row(d_model), row(d_rnn), row(d_attn), _const_spec((1, d_attn)),
                  _layer_spec(lw["w_out"].shape, lw["layer"]), _const_spec((1, d_model))],
        out_specs=[row(d_model), row(d_model)],
        out_shape=[jax.ShapeDtypeStruct((n, d_model), F32), jax.ShapeDtypeStruct((n, d_model), BF16)],
        compiler_params=_params(("parallel",)),
        name="merge",
    )(x, y_rnn, y_attn, lw["g_out_attn"], lw["w_out"], lw["g_mlp"])


def _mlp_kernel(h_ref, hn_ref, wup_ref, wdn_ref, o_ref):
    @pl.when(pl.program_id(1) == 0)
    def _():
        o_ref[...] = h_ref[...]
    up = jnp.maximum(_dot(hn_ref[...], wup_ref[...]), 0.0)
    o_ref[...] += _dot((up * up).astype(BF16), wdn_ref[...])


def _mlp(h, hn, lw, *, tm, tf):
    n, d_model = h.shape
    d_ff = lw["w_up"].shape[2]
    layer = lw["layer"]
    assert n % tm == 0 and d_ff % tf == 0
    return pl.pallas_call(
        _mlp_kernel,
        grid=(n // tm, d_ff // tf),
        in_specs=[pl.BlockSpec((tm, d_model), lambda i, f: (i, 0)),
                  pl.BlockSpec((tm, d_model), lambda i, f: (i, 0)),
                  pl.BlockSpec((None, d_model, tf), lambda i, f: (layer, 0, f)),
                  pl.BlockSpec((None, tf, d_model), lambda i, f: (layer, f, 0))],
        out_specs=pl.BlockSpec((tm, d_model), lambda i, f: (i, 0)),
        out_shape=jax.ShapeDtypeStruct((n, d_model), F32),
        compiler_params=_params(("parallel", "arbitrary")),
        name="mlp",
    )(h, hn, lw["w_up"], lw["w_down"])


def _rope_angles(pos, rope_dim):
    half = rope_dim // 2
    inv_freq = ROPE_THETA ** (-jnp.arange(half, dtype=F32) / half)
    return pos.astype(F32)[:, None] * inv_freq[None, :]


def _token_tables(pos, rope_dim):
    ang = _rope_angles(pos, rope_dim)
    cos, sin = jnp.cos(ang), jnp.sin(ang)
    pad = jnp.zeros((pos.shape[0], LANES - rope_dim), F32)
    return jnp.concatenate([cos, cos, pad], axis=1), jnp.concatenate([-sin, sin, pad], axis=1)


def _key_tables(pos, rope_dim, block):
    ang = _rope_angles(pos, rope_dim).T
    half = rope_dim // 2
    nb = pos.shape[0] // block
    to_blocks = lambda a: a.reshape(half, nb, block).transpose(1, 0, 2)
    return to_blocks(jnp.cos(ang)), to_blocks(jnp.sin(ang))


def _stacked_weights(w_in, w_out, w_up, w_down, rope_dim):
    depth, d_model, _ = w_in.shape
    w_in = jnp.concatenate([w_in, jnp.zeros((depth, d_model, LANES - rope_dim), F32)], axis=2)
    return dict(w_in=w_in.astype(BF16), w_out=w_out.astype(BF16), w_up=w_up.astype(BF16),
                w_down=w_down.astype(BF16))


def _layer_weights(l, stacked, w_qb, w_uk, w_uv, lru_wa, lru_wx, vecs):
    kv_rank, n_heads, nope = w_uk.shape[1:]
    q_rank = w_qb.shape[1]
    d_rnn = lru_wa.shape[1] * lru_wa.shape[2]
    qk_dim = w_qb.shape[2] // n_heads
    rope_dim = qk_dim - nope
    half = rope_dim // 2
    slot = 2 * LANES
    kb = PAGES_PER_BLOCK * LANES
    v = {k: a[l][None, :] for k, a in vecs.items()}
    g_qn, g_kn = v.pop("g_qn"), v.pop("g_kn")
    pad = jnp.zeros((1, LANES - rope_dim), F32)
    lw = dict(v)
    lw.update(stacked)
    lw["layer"] = l
    lw["dims"] = (d_rnn, q_rank, kv_rank, n_heads, nope, rope_dim)
    lw["gq0"], lw["gq1"] = g_qn[:, :nope], jnp.concatenate([g_qn[:, nope:], pad], axis=1)
    lw["gk0"], lw["gk1"] = g_kn[:, :nope], jnp.concatenate([g_kn[:, nope:], pad], axis=1)
    lw["gk_pe1"] = jnp.broadcast_to(g_kn[0, nope:nope + half][:, None], (half, kb))
    lw["gk_pe2"] = jnp.broadcast_to(g_kn[0, nope + half:][:, None], (half, kb))
    wq = w_qb[l].reshape(q_rank, n_heads, qk_dim)
    wq = jnp.concatenate([wq, jnp.zeros((q_rank, n_heads, slot - qk_dim), F32)], axis=2)
    lw["w_qb"] = wq.reshape(q_rank, n_heads * slot).astype(BF16)
    lw["w_uk"] = w_uk[l].reshape(kv_rank, n_heads * nope).astype(BF16)
    lw["w_ukT_h"] = w_uk[l].transpose(1, 2, 0).astype(BF16)
    lw["w_ukT"] = lw["w_ukT_h"].reshape(n_heads * nope, kv_rank)
    lw["w_uv"] = w_uv[l].reshape(kv_rank, -1).astype(BF16)
    lw["w_uv_h"] = w_uv[l].transpose(1, 0, 2).astype(BF16)
    lw["w_ax"] = jnp.concatenate([lru_wa[l], lru_wx[l]], axis=-1).astype(BF16)
    return lw


def _trunk_layer(x_p, x_s, state_conv8, h0_s, page_table, cache_ckv, cache_kpe, lw, tabs, layer, cfg):
    d_rnn, q_rank, kv_rank, n_heads, nope, rope_dim = lw["dims"]
    scale = (nope + rope_dim) ** -0.5
    db = page_table.shape[0]
    t_new = x_s.shape[0] // db

    xr, gate, ckv_p, kpe_p, q, k, v = _prep(x_p, lw, tabs["cos_p"], tabs["sin_p"], prompt=True, tm=cfg["tm"])
    y_rnn, hl_p = _rnn(xr, gate, None, None, lw, chain=True, tt=cfg["tt"])
    y_attn = _flash(q, k, v, n_heads=n_heads, tq=cfg["tq"])
    h, hn = _merge(x_p, y_rnn, y_attn, lw, tm=cfg["tm"])
    y_p = _mlp(h, hn, lw, tm=cfg["tm_mlp"], tf=cfg["tf"])
    conv_p = xr[-(CONV_W - 1):]

    xr, gate, ckv_s, kpe_s, qlat, qpe = _prep(x_s, lw, tabs["cos_s"], tabs["sin_s"], prompt=False,
                                              tm=cfg["tm"])
    y_rnn, hl_s = _rnn(xr, gate, state_conv8, h0_s, lw, chain=False, tt=cfg["tt"])
    olat = _paged(qlat, qpe, ckv_s, kpe_s, page_table, cache_ckv, cache_kpe, lw,
                  (tabs["cos_k"], tabs["sin_k"], tabs["cos_n"], tabs["sin_n"]), layer=layer, scale=scale)
    y_attn = _uvproj(olat, lw["w_uv_h"], t_new=t_new)
    h, hn = _merge(x_s, y_rnn, y_attn, lw, tm=cfg["tm"])
    y_s = _mlp(h, hn, lw, tm=cfg["tm_mlp"], tf=cfg["tf"])
    conv_s = xr.reshape(db, t_new, d_rnn)[:, t_new - (CONV_W - 1):]
    return (y_p, y_s, ckv_p, kpe_p, hl_p[-1], conv_p, ckv_s.reshape(db, t_new, kv_rank),
            kpe_s.reshape(db, t_new, rope_dim), hl_s[:, 0], conv_s)


def kernel(x_prompt, x_sample, cache_ckv, cache_kpe, state_rglru_h, state_conv, page_table, g_mix, w_in,
           conv_w, conv_b, lru_wa, lru_ba, lru_wx, lru_bx, lru_lambda, g_q_a, w_qb, g_kv_a, w_uk, w_uv,
           g_qn, g_kn, g_out_rnn, g_out_attn, w_out, g_mlp, w_up, w_down):
    batch, seq, d_model = x_prompt.shape
    db, t_new, _ = x_sample.shape
    depth = w_in.shape[0]
    assert batch == 1
    n_pages = page_table.shape[1]
    page = cache_ckv.shape[2]
    past = n_pages * page
    rope_dim = cache_kpe.shape[3]
    d_rnn = state_rglru_h.shape[2]
    kb = PAGES_PER_BLOCK * page

    cos_p, sin_p = _token_tables(jnp.arange(seq, dtype=jnp.int32), rope_dim)
    pos_s = past + jnp.arange(t_new, dtype=jnp.int32)
    cos_s, sin_s = _token_tables(jnp.tile(pos_s, db), rope_dim)
    cos_k, sin_k = _key_tables(jnp.arange(past, dtype=jnp.int32), rope_dim, kb)
    cos_n, sin_n = _key_tables(past + jnp.arange(page, dtype=jnp.int32), rope_dim, page)
    tabs = dict(cos_p=cos_p, sin_p=sin_p, cos_s=cos_s, sin_s=sin_s, cos_k=cos_k, sin_k=sin_k,
                cos_n=cos_n[0], sin_n=sin_n[0])
    cfg = dict(tm=min(512, seq, db * t_new), tt=min(32, db * t_new // SUBLANES), tq=min(1024, seq),
               tm_mlp=min(512, seq, db * t_new), tf=min(1024, w_up.shape[2]))

    vecs = dict(g_mix=g_mix, conv_b=conv_b, lru_ba=lru_ba.reshape(depth, d_rnn),
                lru_bx=lru_bx.reshape(depth, d_rnn), lru_lambda=lru_lambda, g_q_a=g_q_a, g_kv_a=g_kv_a,
                g_qn=g_qn, g_kn=g_kn, g_out_rnn=g_out_rnn, g_out_attn=g_out_attn, g_mlp=g_mlp)
    y_p = x_prompt.reshape(seq, d_model)
    y_s = x_sample.reshape(db * t_new, d_model)
    state_conv8 = jnp.pad(state_conv, ((0, 0), (0, 0), (SUBLANES - (CONV_W - 1), 0), (0, 0)))
    cache_kpe = cache_kpe.transpose(0, 1, 3, 2)
    stacked = _stacked_weights(w_in, w_out, w_up, w_down, rope_dim)
    outs = []
    for l in range(depth):
        lw = _layer_weights(l, stacked, w_qb, w_uk, w_uv, lru_wa, lru_wx, vecs)
        lw["conv_w"] = conv_w[l]
        res = _trunk_layer(y_p, y_s, state_conv8[l], state_rglru_h[l][:, None, :], page_table, cache_ckv,
                           cache_kpe, lw, tabs, l, cfg)
        y_p, y_s = res[0], res[1]
        outs.append(res[2:])
    ckv_p, kpe_p, h_p, conv_p, ckv_s, kpe_s, h_s, conv_s = (jnp.stack(a) for a in zip(*outs))
    return (y_p.reshape(batch, seq, d_model), y_s.reshape(db, t_new, d_model),
            ckv_p[:, None], kpe_p[:, None], h_p, conv_p[:, None], ckv_s, kpe_s, h_s, conv_s)
```
